```python
import jax, jax.numpy as jnp
from jax import lax
import numpy as np

D_MODEL = 1024
BATCH = 16
SEQ = 4096
DEPTH = 1

PLE_DIM = 256
D_MIX = D_MODEL
FOX_HEADS = 8
FOX_HEAD_DIM = 64
FOX_WIDTH = FOX_HEADS * FOX_HEAD_DIM
MLSTM_HEADS = 4
MLSTM_HEAD_DIM = 128
MLSTM_WIDTH = MLSTM_HEADS * MLSTM_HEAD_DIM
CONV_WIDTH = 4
D_FF = 4 * D_MODEL
Q_BLOCK = 128
CHUNK = 128
EPS = 1e-6

IN_SIZES = (FOX_WIDTH, FOX_WIDTH, FOX_WIDTH, FOX_HEADS,
            MLSTM_WIDTH, MLSTM_WIDTH, MLSTM_WIDTH, MLSTM_HEADS, MLSTM_HEADS, MLSTM_WIDTH)
IN_COLS = int(sum(IN_SIZES))
IN_SPLITS = tuple(int(s) for s in np.cumsum(IN_SIZES)[:-1])

kernel_name = "hymba_fox_mlstm_sqrelu_ple"


def rmsnorm(x, g):
    xf = x.astype(jnp.float32)
    y = xf * lax.rsqrt(jnp.mean(xf * xf, axis=-1, keepdims=True) + EPS)
    return (y * g.astype(jnp.float32)).astype(x.dtype)


def head_rmsnorm(h, g):
    B, S, H, d = h.shape
    hf = h.astype(jnp.float32)
    y = hf * lax.rsqrt(jnp.mean(hf * hf, axis=-1, keepdims=True) + EPS)
    return y.reshape(B, S, H * d) * g.astype(jnp.float32)


def causal_conv(u, w):
    K = w.shape[0]
    S = u.shape[1]
    up = jnp.pad(u, ((0, 0), (K - 1, 0), (0, 0)))
    out = up[:, 0:S] * w[0]
    for j in range(1, K):
        out = out + up[:, j:j + S] * w[j]
    return out


def fox_attention(q, k, v, f_pre):
    B, S, H, d = q.shape
    nb = S // Q_BLOCK
    log_f = jax.nn.log_sigmoid(f_pre.astype(jnp.float32))
    c = jnp.cumsum(log_f, axis=1).transpose(0, 2, 1)
    kh = k.transpose(0, 2, 1, 3)
    vh = v.transpose(0, 2, 1, 3)
    qb = q.reshape(B, nb, Q_BLOCK, H, d).transpose(1, 0, 3, 2, 4)
    cb = c.reshape(B, H, nb, Q_BLOCK).transpose(2, 0, 1, 3)
    kpos = jnp.arange(S)
    scale = d ** -0.5

    def block(args):
        qi, ci, i = args
        s = jnp.einsum('bhqd,bhkd->bhqk', qi, kh).astype(jnp.float32) * scale
        s = s + ci[..., :, None] - c[:, :, None, :]
        qpos = i * Q_BLOCK + jnp.arange(Q_BLOCK)
        s = jnp.where(kpos[None, :] <= qpos[:, None], s, -jnp.inf)
        pr = jax.nn.softmax(s, axis=-1).astype(vh.dtype)
        return jnp.einsum('bhqk,bhkd->bhqd', pr, vh)

    o = lax.map(block, (qb, cb, jnp.arange(nb)))
    return o.transpose(1, 0, 3, 2, 4).reshape(B, S, H, d)


def mlstm_chunkwise(q, k, v, i_pre, f_pre):
    B, S, H, d = q.shape
    L = CHUNK
    nc = S // L
    f32 = jnp.float32

    def to_chunks(a):
        a = a.reshape((B, nc, L, H) + a.shape[3:])
        return jnp.moveaxis(a, (1, 3), (0, 2))

    qc = to_chunks(q.astype(f32))
    kc = to_chunks(k.astype(f32) * (d ** -0.5))
    vc = to_chunks(v.astype(f32))
    ic = to_chunks(i_pre.astype(f32))
    fc = to_chunks(jax.nn.log_sigmoid(f_pre.astype(f32)))
    causal = jnp.tril(jnp.ones((L, L), dtype=bool))

    def step(carry, xs):
        C, n, m = carry
        qi, ki, vi, ii, fi = xs
        b = jnp.cumsum(fi, axis=-1)
        Dm = b[..., :, None] - b[..., None, :] + ii[..., None, :]
        Dm = jnp.where(causal, Dm, -jnp.inf)
        inter = b + m[..., None]
        mt = jnp.maximum(inter, jnp.max(Dm, axis=-1))
        w_inter = jnp.exp(inter - mt)
        w_intra = jnp.exp(Dm - mt[..., None])
        sqk = jnp.einsum('bhtd,bhsd->bhts', qi, ki) * w_intra
        num = (w_inter[..., None] * jnp.einsum('bhtd,bhde->bhte', qi, C)
               + jnp.einsum('bhts,bhse->bhte', sqk, vi))
        den = w_inter * jnp.einsum('bhtd,bhd->bht', qi, n) + jnp.sum(sqk, axis=-1)
        h = num / jnp.maximum(jnp.abs(den), jnp.exp(-mt))[..., None]
        bL = b[..., -1]
        g = bL[..., None] - b + ii
        m_new = jnp.maximum(bL + m, jnp.max(g, axis=-1))
        decay = jnp.exp(bL + m - m_new)
        wk = jnp.exp(g - m_new[..., None])[..., None] * ki
        C_new = decay[..., None, None] * C + jnp.einsum('bhsd,bhse->bhde', wk, vi)
        n_new = decay[..., None] * n + jnp.sum(wk, axis=2)
        return (C_new, n_new, m_new), h

    init = (jnp.zeros((B, H, d, d), f32), jnp.zeros((B, H, d), f32), jnp.zeros((B, H), f32))
    _, hs = lax.scan(step, init, (qc, kc, vc, ic, fc))
    hs = jnp.moveaxis(hs, (0, 2), (1, 3)).reshape(B, S, H, d)
    return hs


def setup_inputs(seed: int = 0) -> dict:
    key = jax.random.key(seed)
    ks = jax.random.split(key, 20)
    f32 = jnp.float32
    nrm = lambda k, shape, s: jax.random.normal(k, shape, f32) * s
    gain = lambda k, shape: 1.0 + 0.02 * jax.random.normal(k, shape, f32)
    return {
        "x": jax.random.normal(ks[0], (BATCH, SEQ, D_MODEL), f32),
        "p": jax.random.normal(ks[1], (DEPTH, BATCH, SEQ, PLE_DIM), f32),
        "w_in": nrm(ks[2], (DEPTH, D_MODEL, IN_COLS), D_MODEL ** -0.5),
        "b_fox_f": 3.0 + 0.5 * jax.random.normal(ks[3], (DEPTH, FOX_HEADS), f32),
        "b_mlstm_i": nrm(ks[4], (DEPTH, MLSTM_HEADS), 0.1),
        "b_mlstm_f": jnp.linspace(3.0, 6.0, MLSTM_HEADS, dtype=f32)[None, :]
                     + 0.1 * jax.random.normal(ks[5], (DEPTH, MLSTM_HEADS), f32),
        "w_conv": nrm(ks[6], (DEPTH, CONV_WIDTH, 2 * MLSTM_WIDTH), CONV_WIDTH ** -0.5),
        "g_mix": gain(ks[7], (DEPTH, D_MODEL)),
        "g_fox_out": gain(ks[8], (DEPTH, FOX_WIDTH)),
        "g_mlstm_out": gain(ks[9], (DEPTH, MLSTM_WIDTH)),
        "w_out": nrm(ks[10], (DEPTH, D_MIX, D_MODEL), D_MIX ** -0.5),
        "g_mlp": gain(ks[11], (DEPTH, D_MODEL)),
        "w_up": nrm(ks[12], (DEPTH, D_MODEL, D_FF), D_MODEL ** -0.5),
        "w_down": nrm(ks[13], (DEPTH, D_FF, D_MODEL), D_FF ** -0.5),
        "w_ple": nrm(ks[14], (DEPTH, PLE_DIM, D_MODEL), PLE_DIM ** -0.5),
        "g_ple": gain(ks[15], (DEPTH, D_MODEL)),
        "w_ple_gate": nrm(ks[16], (DEPTH, D_MODEL, D_MODEL), D_MODEL ** -0.5),
        "g_final": gain(ks[17], (D_MODEL,)),
    }


def reference(x, p, w_in, b_fox_f, b_mlstm_i, b_mlstm_f, w_conv, g_mix, g_fox_out,
              g_mlstm_out, w_out, g_mlp, w_up, w_down, w_ple, g_ple, w_ple_gate, g_final):
    B, S, _ = x.shape
    for i in range(DEPTH):
        h = rmsnorm(x, g_mix[i])
        z = h @ w_in[i]
        (fq, fk, fv, ff, mq, mk, mv, mi, mf, mo) = jnp.split(z, IN_SPLITS, axis=-1)

        fox = fox_attention(fq.reshape(B, S, FOX_HEADS, FOX_HEAD_DIM),
                            fk.reshape(B, S, FOX_HEADS, FOX_HEAD_DIM),
                            fv.reshape(B, S, FOX_HEADS, FOX_HEAD_DIM),
                            ff + b_fox_f[i])
        fox_out = head_rmsnorm(fox, g_fox_out[i])

        qk = jax.nn.silu(causal_conv(jnp.concatenate([mq, mk], axis=-1), w_conv[i]))
        mq_c, mk_c = jnp.split(qk, 2, axis=-1)
        ml = mlstm_chunkwise(mq_c.reshape(B, S, MLSTM_HEADS, MLSTM_HEAD_DIM),
                             mk_c.reshape(B, S, MLSTM_HEADS, MLSTM_HEAD_DIM),
                             mv.reshape(B, S, MLSTM_HEADS, MLSTM_HEAD_DIM),
                             mi + b_mlstm_i[i], mf + b_mlstm_f[i])
        ml_out = head_rmsnorm(ml, g_mlstm_out[i]) * jax.nn.sigmoid(mo.astype(jnp.float32))

        mix = jnp.concatenate([fox_out, ml_out], axis=-1).astype(x.dtype)
        x = x + mix @ w_out[i]

        hm = rmsnorm(x, g_mlp[i])
        x = x + jnp.square(jax.nn.relu(hm @ w_up[i])) @ w_down[i]

        gate = jax.nn.sigmoid(rmsnorm(x, g_ple[i]) @ w_ple_gate[i])
        x = x + gate * (p[i] @ w_ple[i])
    return rmsnorm(x, g_final)
```

```python
import functools
import math

import jax
import jax.numpy as jnp
from jax import lax
from jax.experimental import pallas as pl
from jax.experimental.pallas import tpu as pltpu

F32 = jnp.float32
BF16 = jnp.bfloat16

EPS = 1e-6
FOX_HEADS = 8
FOX_HEAD_DIM = 64
FOX_WIDTH = FOX_HEADS * FOX_HEAD_DIM
MLSTM_HEADS = 4
MLSTM_HEAD_DIM = 128
MLSTM_WIDTH = MLSTM_HEADS * MLSTM_HEAD_DIM
CONV_WIDTH = 4
CHUNK = 128
LANES = 128
GATE_COLS = LANES
LOG2E = 1.4426950408889634
NEG_BIG = -1e30
VMEM_LIMIT = 56 * 1024 * 1024

COL_FOX_F = 0
COL_ML_I = FOX_HEADS
COL_ML_F = FOX_HEADS + MLSTM_HEADS


def _split3(x):
    hi = x.astype(BF16)
    r1 = x - hi.astype(F32)
    mid = r1.astype(BF16)
    lo = (r1 - mid.astype(F32)).astype(BF16)
    return hi, mid, lo


def _log_sigmoid(x):
    return jnp.minimum(x, 0.0) - jnp.log1p(jnp.exp(-jnp.abs(x)))


def _sigmoid(x):
    return 1.0 / (1.0 + jnp.exp(-x))


def _inproj_kernel(x_ref, g_ref, wn_ref, wt_ref, bias_ref, tri_ref, place_ref,
                   fk_ref, mz_ref, qvt_ref, gm_ref, gmt_ref, caug_ref, carry_ref, *, tm, qscale):
    @pl.when(pl.program_id(1) == 0)
    def _():
        carry_ref[...] = jnp.zeros_like(carry_ref)

    x = x_ref[0]
    ms = jnp.mean(x * x, axis=-1, keepdims=True)
    h = (x * lax.rsqrt(ms + EPS) * g_ref[...]).astype(BF16)

    z = jnp.dot(h, wn_ref[...], preferred_element_type=F32)
    fk_ref[0] = z[:, :FOX_WIDTH].astype(BF16)
    mz_ref[0] = z[:, FOX_WIDTH:FOX_WIDTH + 4 * MLSTM_WIDTH].astype(BF16)

    nt = (((1,), (1,)), ((), ()))
    qt = lax.dot_general(wt_ref[:FOX_WIDTH, :], h, nt, preferred_element_type=F32)
    qvt_ref[0, :FOX_WIDTH, :] = (qt * qscale).astype(BF16)
    vt = lax.dot_general(wt_ref[FOX_WIDTH:, :], h, nt, preferred_element_type=F32)
    qvt_ref[0, FOX_WIDTH:, :] = vt.astype(BF16)

    g = z[:, FOX_WIDTH + 4 * MLSTM_WIDTH:] + bias_ref[...]
    logf = _log_sigmoid(g)
    hi, mid, lo = _split3(logf)
    cs3 = jnp.dot(tri_ref[...], jnp.concatenate([hi, mid, lo], axis=1), preferred_element_type=F32)
    cs = cs3[:, :GATE_COLS] + cs3[:, GATE_COLS:2 * GATE_COLS] + cs3[:, 2 * GATE_COLS:]
    pieces = []
    for c in range(tm // CHUNK):
        blk = cs[c * CHUNK:(c + 1) * CHUNK]
        if c > 0:
            blk = blk - cs[c * CHUNK - 1:c * CHUNK]
        pieces.append(blk)
    bloc = jnp.concatenate(pieces, axis=0)
    cfox = (carry_ref[...] + cs) * LOG2E
    carry_ref[...] = carry_ref[...] + cs[tm - 1:tm]

    col = lax.broadcasted_iota(jnp.int32, (tm, GATE_COLS), 1)
    gm = jnp.where(col < COL_ML_I, cfox, jnp.where(col < COL_ML_F, g, bloc))
    gm_ref[0] = gm
    gmt_ref[0] = gm.T
    chi, cmid, clo = _split3(cfox)
    caug = jnp.dot(jnp.concatenate([chi, cmid, clo], axis=1), place_ref[...], preferred_element_type=F32)
    caug_ref[0] = caug.astype(BF16)


def _inproj(x, g_mix, w_nat, w_t, bias, *, tm):
    B, S, D = x.shape
    n_nat = w_nat.shape[1]
    qscale = FOX_HEAD_DIM ** -0.5 * LOG2E
    tri = (jnp.arange(tm)[:, None] >= jnp.arange(tm)[None, :]).astype(BF16)
    rows = jnp.arange(3 * GATE_COLS)
    piece, head = rows // GATE_COLS, rows % GATE_COLS
    place = ((head[:, None] < FOX_HEADS) & (jnp.arange(GATE_COLS)[None, :] == (3 * head + piece)[:, None])).astype(BF16)
    const = lambda *_: (0, 0)
    return pl.pallas_call(
        functools.partial(_inproj_kernel, tm=tm, qscale=qscale),
        grid=(B, S // tm),
        in_specs=[
            pl.BlockSpec((1, tm, D), lambda b, s: (b, s, 0)),
            pl.BlockSpec((1, D), const),
            pl.BlockSpec((D, n_nat), const),
            pl.BlockSpec((2 * FOX_WIDTH, D), const),
            pl.BlockSpec((1, GATE_COLS), const),
            pl.BlockSpec((tm, tm), const),
            pl.BlockSpec((3 * GATE_COLS, GATE_COLS), const),
        ],
        out_specs=[
            pl.BlockSpec((1, tm, FOX_WIDTH), lambda b, s: (b, s, 0)),
            pl.BlockSpec((1, tm, 4 * MLSTM_WIDTH), lambda b, s: (b, s, 0)),
            pl.BlockSpec((1, 2 * FOX_WIDTH, tm), lambda b, s: (b, 0, s)),
            pl.BlockSpec((1, tm, GATE_COLS), lambda b, s: (b, s, 0)),
            pl.BlockSpec((1, GATE_COLS, tm), lambda b, s: (b, 0, s)),
            pl.BlockSpec((1, tm, GATE_COLS), lambda b, s: (b, s, 0)),
        ],
        out_shape=[
            jax.ShapeDtypeStruct((B, S, FOX_WIDTH), BF16),
            jax.ShapeDtypeStruct((B, S, 4 * MLSTM_WIDTH), BF16),
            jax.ShapeDtypeStruct((B, 2 * FOX_WIDTH, S), BF16),
            jax.ShapeDtypeStruct((B, S, GATE_COLS), F32),
            jax.ShapeDtypeStruct((B, GATE_COLS, S), F32),
            jax.ShapeDtypeStruct((B, S, GATE_COLS), BF16),
        ],
        scratch_shapes=[pltpu.VMEM((1, GATE_COLS), F32)],
        compiler_params=pltpu.CompilerParams(
            dimension_semantics=("arbitrary", "arbitrary"), vmem_limit_bytes=VMEM_LIMIT),
        name="inproj",
    )(x, g_mix, w_nat, w_t, bias, tri, place)


def _fox_kernel(qt_ref, vt_ref, k_ref, caug_ref, gain_ref, o_ref, *, tq):
    tk = tq
    j = pl.program_id(1)
    qi = pl.program_id(2)
    qt = qt_ref[0]
    row = lax.broadcasted_iota(jnp.int32, (2 * FOX_HEAD_DIM, tq), 0)
    rhs = []
    for r in range(2):
        head = 2 * j + r
        qm = jnp.where(row >= FOX_HEAD_DIM * r, jnp.where(row < FOX_HEAD_DIM * (r + 1), qt, 0), 0)
        sel = jnp.where(row >= 3 * head, jnp.where(row < 3 * head + 3, -1.0, 0.0), 0.0).astype(BF16)
        rhs.append(jnp.concatenate([qm.astype(BF16), sel], axis=0))

    kpos = lax.broadcasted_iota(jnp.int32, (tk, tq), 0)
    qpos = lax.broadcasted_iota(jnp.int32, (tk, tq), 1)
    causal = kpos <= qpos

    def step(t, carry, masked):
        ks = pl.multiple_of(t * tk, tk)
        lhs = jnp.concatenate([k_ref[0, pl.ds(ks, tk), :], caug_ref[0, pl.ds(ks, tk), :]], axis=1)
        new = []
        for r in range(2):
            m, l, acc = carry[r]
            st = jnp.dot(lhs, rhs[r], preferred_element_type=F32)
            if masked:
                st = jnp.where(causal, st, NEG_BIG)
            mn = jnp.maximum(m, jnp.max(st, axis=0, keepdims=True))
            alpha = jnp.exp2(m - mn)
            p = jnp.exp2(st - mn)
            l = alpha * l + jnp.sum(p, axis=0, keepdims=True)
            vt = vt_ref[0, FOX_HEAD_DIM * r:FOX_HEAD_DIM * (r + 1), pl.ds(ks, tk)]
            acc = alpha * acc + jnp.dot(vt, p.astype(BF16), preferred_element_type=F32)
            new.append((mn, l, acc))
        return tuple(new)

    init = tuple((jnp.full((1, tq), NEG_BIG, F32), jnp.zeros((1, tq), F32),
                  jnp.zeros((FOX_HEAD_DIM, tq), F32)) for _ in range(2))
    carry = lax.fori_loop(0, qi, lambda t, c: step(t, c, False), init)
    carry = step(qi, carry, True)

    outs = []
    for r in range(2):
        _, l, acc = carry[r]
        o = acc / l
        ms = jnp.mean(o * o, axis=0, keepdims=True)
        outs.append(o * lax.rsqrt(ms + EPS))
    y = jnp.concatenate(outs, axis=0) * gain_ref[...]
    o_ref[0] = y.T.astype(BF16)


def _fox(qvt, fk, caug, gain_b, *, tq):
    B, S, _ = fk.shape
    pair = 2 * FOX_HEAD_DIM
    n_pairs = FOX_HEADS // 2
    return pl.pallas_call(
        functools.partial(_fox_kernel, tq=tq),
        grid=(B, n_pairs, S // tq),
        in_specs=[
            pl.BlockSpec((1, pair, tq), lambda b, j, q: (b, j, q)),
            pl.BlockSpec((1, pair, S), lambda b, j, q: (b, n_pairs + j, 0)),
            pl.BlockSpec((1, S, pair), lambda b, j, q: (b, 0, j)),
            pl.BlockSpec((1, S, GATE_COLS), lambda b, j, q: (b, 0, 0)),
            pl.BlockSpec((pair, tq), lambda b, j, q: (j, 0)),
        ],
        out_specs=pl.BlockSpec((1, tq, pair), lambda b, j, q: (b, q, j)),
        out_shape=jax.ShapeDtypeStruct((B, S, FOX_WIDTH), BF16),
        compiler_params=pltpu.CompilerParams(
            dimension_semantics=("arbitrary", "arbitrary", "arbitrary"), vmem_limit_bytes=VMEM_LIMIT),
        name="fox",
    )(qvt, qvt, fk, caug, gain_b)


def _mlstm_kernel(mz_ref, gm_ref, gmt_ref, wconv_ref, gain_ref, o_ref,
                  ubuf_ref, state_ref, mstate_ref, *, tm):
    d = MLSTM_HEAD_DIM
    L = CHUNK
    W = MLSTM_WIDTH
    pad = 8

    @pl.when(pl.program_id(1) == 0)
    def _():
        ubuf_ref[0:pad, :] = jnp.zeros((pad, 2 * W), F32)
        state_ref[...] = jnp.zeros_like(state_ref)
        mstate_ref[...] = jnp.zeros_like(mstate_ref)

    ubuf_ref[pad:pad + tm, :] = mz_ref[0, :, :2 * W].astype(F32)
    conv = ubuf_ref[pad:pad + tm, :] * wconv_ref[CONV_WIDTH - 1:CONV_WIDTH, :]
    for jj in range(CONV_WIDTH - 1):
        shift = CONV_WIDTH - 1 - jj
        conv = conv + ubuf_ref[pad - shift:pad - shift + tm, :] * wconv_ref[jj:jj + 1, :]
    ubuf_ref[0:pad, :] = ubuf_ref[tm:tm + pad, :]
    qk = conv * _sigmoid(conv)
    q_all = qk[:, :W].astype(BF16)
    k_all = qk[:, W:] * (d ** -0.5)

    trow = lax.broadcasted_iota(jnp.int32, (L, L), 0)
    scol = lax.broadcasted_iota(jnp.int32, (L, L), 1)
    causal = scol <= trow
    lane = lax.broadcasted_iota(jnp.int32, (L, LANES), 1)
    ones_col = jnp.where(lane == 0, 1.0, 0.0).astype(BF16)
    nt = (((1,), (1,)), ((), ()))

    for c in range(tm // L):
        rs = slice(c * L, (c + 1) * L)
        for hd in range(MLSTM_HEADS):
            cs_ = slice(hd * d, (hd + 1) * d)
            q = q_all[rs, cs_]
            kf = k_all[rs, cs_]
            k = kf.astype(BF16)
            v = mz_ref[0, rs, 2 * W + hd * d:2 * W + (hd + 1) * d]
            v_aug = jnp.concatenate([v, ones_col], axis=1)
            i_col = gm_ref[0, rs, COL_ML_I + hd:COL_ML_I + hd + 1]
            b_col = gm_ref[0, rs, COL_ML_F + hd:COL_ML_F + hd + 1]
            i_row = gmt_ref[0, COL_ML_I + hd:COL_ML_I + hd + 1, rs]
            b_row = gmt_ref[0, COL_ML_F + hd:COL_ML_F + hd + 1, rs]
            m_prev = mstate_ref[hd]
            m_sc = m_prev[:, :1]
            c_aug = state_ref[hd]

            dm = jnp.where(causal, b_col - (b_row - i_row), NEG_BIG)
            inter = b_col + m_sc
            mt = jnp.maximum(inter, jnp.max(dm, axis=-1, keepdims=True))
            w_inter = jnp.exp(inter - mt)
            w_intra = jnp.exp(dm - mt)
            sqk = lax.dot_general(q, k, nt, preferred_element_type=F32) * w_intra
            nd = (w_inter * jnp.dot(q, c_aug.astype(BF16), preferred_element_type=F32)
                  + jnp.dot(sqk.astype(BF16), v_aug, preferred_element_type=F32))
            den = nd[:, d:d + 1]
            hcur = nd[:, :d] / jnp.maximum(jnp.abs(den), jnp.exp(-mt))

            b_last = b_col[L - 1:L, :]
            gcol = b_last - b_col + i_col
            m_new = jnp.maximum(b_last + m_sc, jnp.max(gcol, axis=0, keepdims=True))
            decay = jnp.exp(b_last + m_sc - m_new)
            wk = jnp.exp(gcol - m_new) * kf
            upd = jnp.dot(wk.T.astype(BF16), v_aug, preferred_element_type=F32)
            state_ref[hd] = decay * c_aug + upd
            mstate_ref[hd] = jnp.broadcast_to(m_new, (1, LANES))

            ms = jnp.mean(hcur * hcur, axis=-1, keepdims=True)
            og = mz_ref[0, rs, 3 * W + hd * d:3 * W + (hd + 1) * d].astype(F32)
            y = hcur * lax.rsqrt(ms + EPS) * gain_ref[:, cs_] * _sigmoid(og)
            o_ref[0, rs, cs_] = y.astype(BF16)


def _mlstm(mz, gm, gmt, w_conv, gain, *, tm):
    B, S, _ = mz.shape
    W = MLSTM_WIDTH
    const = lambda *_: (0, 0)
    return pl.pallas_call(
        functools.partial(_mlstm_kernel, tm=tm),
        grid=(B, S // tm),
        in_specs=[
            pl.BlockSpec((1, tm, 4 * W), lambda b, s: (b, s, 0)),
            pl.BlockSpec((1, tm, GATE_COLS), lambda b, s: (b, s, 0)),
            pl.BlockSpec((1, GATE_COLS, tm), lambda b, s: (b, 0, s)),
            pl.BlockSpec((CONV_WIDTH, 2 * W), const),
            pl.BlockSpec((1, W), const),
        ],
        out_specs=pl.BlockSpec((1, tm, W), lambda b, s: (b, s, 0)),
        out_shape=jax.ShapeDtypeStruct((B, S, W), BF16),
        scratch_shapes=[
            pltpu.VMEM((tm + 8, 2 * W), F32),
            pltpu.VMEM((MLSTM_HEADS, MLSTM_HEAD_DIM, 2 * MLSTM_HEAD_DIM), F32),
            pltpu.VMEM((MLSTM_HEADS, 1, LANES), F32),
        ],
        compiler_params=pltpu.CompilerParams(
            dimension_semantics=("arbitrary", "arbitrary"), vmem_limit_bytes=VMEM_LIMIT),
        name="mlstm",
    )(mz, gm, gmt, w_conv, gain)


def _rms(x, g):
    ms = jnp.mean(x * x, axis=-1, keepdims=True)
    return x * lax.rsqrt(ms + EPS) * g


def _tail_kernel(x_ref, fox_ref, ml_ref, p_ref, wo_ref, wup_ref, wdn_ref, wpg_ref, wple_ref,
                 gmlp_ref, gple_ref, gfin_ref, o_ref, *, ff_chunk):
    mix = jnp.concatenate([fox_ref[...], ml_ref[...]], axis=1)
    x1 = x_ref[...] + jnp.dot(mix, wo_ref[...], preferred_element_type=F32)
    hm = _rms(x1, gmlp_ref[...]).astype(BF16)
    x2 = x1
    d_ff = wup_ref.shape[1]
    for f in range(d_ff // ff_chunk):
        fs = slice(f * ff_chunk, (f + 1) * ff_chunk)
        u = jnp.maximum(jnp.dot(hm, wup_ref[:, fs], preferred_element_type=F32), 0.0)
        x2 = x2 + jnp.dot((u * u).astype(BF16), wdn_ref[fs, :], preferred_element_type=F32)
    hg = _rms(x2, gple_ref[...]).astype(BF16)
    gate = _sigmoid(jnp.dot(hg, wpg_ref[...], preferred_element_type=F32))
    pe = jnp.dot(p_ref[...].astype(BF16), wple_ref[...], preferred_element_type=F32)
    x3 = x2 + gate * pe
    o_ref[...] = _rms(x3, gfin_ref[...])


def _tail(x2d, fox2d, ml2d, p2d, wo, wup, wdn, wpg, wple, gmlp, gple, gfin, *, tm):
    N, D = x2d.shape
    const = lambda i: (0, 0)
    once = pl.Buffered(1)
    wspec = lambda a: pl.BlockSpec(a.shape, const, pipeline_mode=once)
    return pl.pallas_call(
        functools.partial(_tail_kernel, ff_chunk=1024),
        grid=(N // tm,),
        in_specs=[
            pl.BlockSpec((tm, D), lambda i: (i, 0)),
            pl.BlockSpec((tm, fox2d.shape[1]), lambda i: (i, 0)),
            pl.BlockSpec((tm, ml2d.shape[1]), lambda i: (i, 0)),
            pl.BlockSpec((tm, p2d.shape[1]), lambda i: (i, 0)),
            wspec(wo), wspec(wup), wspec(wdn), wspec(wpg), wspec(wple),
            wspec(gmlp), wspec(gple), wspec(gfin),
        ],
        out_specs=pl.BlockSpec((tm, D), lambda i: (i, 0)),
        out_shape=jax.ShapeDtypeStruct((N, D), F32),
        compiler_params=pltpu.CompilerParams(
            dimension_semantics=("arbitrary",), vmem_limit_bytes=VMEM_LIMIT),
        name="tail",
    )(x2d, fox2d, ml2d, p2d, wo, wup, wdn, wpg, wple, gmlp, gple, gfin)


def _layer(x, p_i, w_in, b_fox_f, b_mlstm_i, b_mlstm_f, w_conv, g_mix, g_fox_out, g_mlstm_out,
           w_out, g_mlp, w_up, w_down, w_ple, g_ple, w_ple_gate, g_final):
    B, S, D = x.shape
    fw, mw = FOX_WIDTH, MLSTM_WIDTH
    o = 0
    cols = {}
    for name, size in (("fq", fw), ("fk", fw), ("fv", fw), ("ff", FOX_HEADS), ("mq", mw), ("mk", mw),
                       ("mv", mw), ("mi", MLSTM_HEADS), ("mf", MLSTM_HEADS), ("mo", mw)):
        cols[name] = w_in[:, o:o + size]
        o += size
    gate_w = jnp.concatenate([cols["ff"], cols["mi"], cols["mf"]], axis=1)
    gate_w = jnp.pad(gate_w, ((0, 0), (0, GATE_COLS - gate_w.shape[1])))
    w_nat = jnp.concatenate([cols["fk"], cols["mq"], cols["mk"], cols["mv"], cols["mo"], gate_w], axis=1).astype(BF16)
    w_t = jnp.concatenate([cols["fq"], cols["fv"]], axis=1).T.astype(BF16)
    bias = jnp.concatenate([b_fox_f, b_mlstm_i, b_mlstm_f])
    bias = jnp.pad(bias, (0, GATE_COLS - bias.shape[0]))[None, :].astype(F32)

    fk, mz, qvt, gm, gmt, caug = _inproj(x, g_mix[None, :], w_nat, w_t, bias, tm=512)

    tq = 256
    gain_b = jnp.broadcast_to(g_fox_out.astype(F32)[:, None], (fw, tq))
    fox = _fox(qvt, fk, caug, gain_b, tq=tq)
    ml = _mlstm(mz, gm, gmt, w_conv.astype(F32), g_mlstm_out[None, :].astype(F32), tm=512)

    out = _tail(x.reshape(B * S, D), fox.reshape(B * S, fw), ml.reshape(B * S, mw),
                p_i.reshape(B * S, p_i.shape[-1]),
                w_out.astype(BF16), w_up.astype(BF16), w_down.astype(BF16),
                w_ple_gate.astype(BF16), w_ple.astype(BF16),
                g_mlp[None, :], g_ple[None, :], g_final[None, :], tm=512)
    return out.reshape(B, S, D)


def kernel(x, p, w_in, b_fox_f, b_mlstm_i, b_mlstm_f, w_conv, g_mix, g_fox_out, g_mlstm_out, w_out,
           g_mlp, w_up, w_down, w_ple, g_ple, w_ple_gate, g_final):
    assert w_in.shape[0] == 1, "single-layer kernel: the final RMSNorm is fused into the layer"
    return _layer(x, p[0], w_in[0], b_fox_f[0], b_mlstm_i[0], b_mlstm_f[0], w_conv[0], g_mix[0],
                  g_fox_out[0], g_mlstm_out[0], w_out[0], g_mlp[0], w_up[0], w_down[0], w_ple[0],
                  g_ple[0], w_ple_gate[0], g_final)
```

```python
import functools
import math

import jax
import jax.numpy as jnp
from jax import lax
from jax.experimental import pallas as pl
from jax.experimental.pallas import tpu as pltpu

F32 = jnp.float32
BF16 = jnp.bfloat16

EPS = 1e-6
FOX_HEADS = 8
FOX_HEAD_DIM = 64
FOX_WIDTH = FOX_HEADS * FOX_HEAD_DIM
ACC_ROWS = FOX_HEAD_DIM + 16
MLSTM_HEADS = 4
MLSTM_HEAD_DIM = 128
MLSTM_WIDTH = MLSTM_HEADS * MLSTM_HEAD_DIM
CONV_WIDTH = 4
CHUNK = 128
LANES = 128
GATE_COLS = LANES
LOG2E = 1.4426950408889634
NEG_BIG = -1e30
VMEM_LIMIT = 56 * 1024 * 1024

COL_FOX_F = 0
COL_ML_I = FOX_HEADS
COL_ML_F = FOX_HEADS + MLSTM_HEADS


def _split3(x):
    hi = x.astype(BF16)
    r1 = x - hi.astype(F32)
    mid = r1.astype(BF16)
    lo = (r1 - mid.astype(F32)).astype(BF16)
    return hi, mid, lo


def _log_sigmoid(x):
    return jnp.minimum(x, 0.0) - jnp.log1p(jnp.exp(-jnp.abs(x)))


def _sigmoid(x):
    return 1.0 / (1.0 + jnp.exp(-x))


def _inproj_kernel(x_ref, g_ref, wn_ref, wt_ref, bias_ref, tri_ref, place_ref,
                   fk_ref, mz_ref, qvt_ref, gm_ref, gmt_ref, caug_ref, carry_ref, *, tm, qscale):
    @pl.when(pl.program_id(1) == 0)
    def _():
        carry_ref[...] = jnp.zeros_like(carry_ref)

    x = x_ref[0]
    ms = jnp.mean(x * x, axis=-1, keepdims=True)
    h = (x * lax.rsqrt(ms + EPS) * g_ref[...]).astype(BF16)

    z = jnp.dot(h, wn_ref[...], preferred_element_type=F32)
    fk_ref[0] = z[:, :FOX_WIDTH].astype(BF16)
    mz_ref[0] = z[:, FOX_WIDTH:FOX_WIDTH + 4 * MLSTM_WIDTH].astype(BF16)

    nt = (((1,), (1,)), ((), ()))
    qt = lax.dot_general(wt_ref[:FOX_WIDTH, :], h, nt, preferred_element_type=F32)
    qvt_ref[0, :FOX_WIDTH, :] = (qt * qscale).astype(BF16)
    vt = lax.dot_general(wt_ref[FOX_WIDTH:, :], h, nt, preferred_element_type=F32)
    qvt_ref[0, FOX_WIDTH:, :] = vt.astype(BF16)

    g = z[:, FOX_WIDTH + 4 * MLSTM_WIDTH:] + bias_ref[...]
    logf = _log_sigmoid(g)
    hi, mid, lo = _split3(logf)
    cs3 = jnp.dot(tri_ref[...], jnp.concatenate([hi, mid, lo], axis=1), preferred_element_type=F32)
    cs = cs3[:, :GATE_COLS] + cs3[:, GATE_COLS:2 * GATE_COLS] + cs3[:, 2 * GATE_COLS:]
    pieces = []
    for c in range(tm // CHUNK):
        blk = cs[c * CHUNK:(c + 1) * CHUNK]
        if c > 0:
            blk = blk - cs[c * CHUNK - 1:c * CHUNK]
        pieces.append(blk)
    bloc = jnp.concatenate(pieces, axis=0)
    cfox = (carry_ref[...] + cs) * LOG2E
    carry_ref[...] = carry_ref[...] + cs[tm - 1:tm]

    col = lax.broadcasted_iota(jnp.int32, (tm, GATE_COLS), 1)
    gm = jnp.where(col < COL_ML_I, cfox, jnp.where(col < COL_ML_F, g, bloc))
    gm_ref[0] = gm
    gmt_ref[0] = gm.T
    chi, cmid, clo = _split3(cfox)
    caug = jnp.dot(jnp.concatenate([chi, cmid, clo], axis=1), place_ref[...], preferred_element_type=F32)
    caug_ref[0] = caug.astype(BF16)


def _inproj(x, g_mix, w_nat, w_t, bias, *, tm):
    B, S, D = x.shape
    n_nat = w_nat.shape[1]
    qscale = FOX_HEAD_DIM ** -0.5 * LOG2E
    tri = (jnp.arange(tm)[:, None] >= jnp.arange(tm)[None, :]).astype(BF16)
    rows = jnp.arange(3 * GATE_COLS)
    piece, head = rows // GATE_COLS, rows % GATE_COLS
    place = ((head[:, None] < FOX_HEADS) & (jnp.arange(GATE_COLS)[None, :] == (3 * head + piece)[:, None])).astype(BF16)
    const = lambda *_: (0, 0)
    return pl.pallas_call(
        functools.partial(_inproj_kernel, tm=tm, qscale=qscale),
        grid=(B, S // tm),
        in_specs=[
            pl.BlockSpec((1, tm, D), lambda b, s: (b, s, 0)),
            pl.BlockSpec((1, D), const),
            pl.BlockSpec((D, n_nat), const),
            pl.BlockSpec((2 * FOX_WIDTH, D), const),
            pl.BlockSpec((1, GATE_COLS), const),
            pl.BlockSpec((tm, tm), const),
            pl.BlockSpec((3 * GATE_COLS, GATE_COLS), const),
        ],
        out_specs=[
            pl.BlockSpec((1, tm, FOX_WIDTH), lambda b, s: (b, s, 0)),
            pl.BlockSpec((1, tm, 4 * MLSTM_WIDTH), lambda b, s: (b, s, 0)),
            pl.BlockSpec((1, 2 * FOX_WIDTH, tm), lambda b, s: (b, 0, s)),
            pl.BlockSpec((1, tm, GATE_COLS), lambda b, s: (b, s, 0)),
            pl.BlockSpec((1, GATE_COLS, tm), lambda b, s: (b, 0, s)),
            pl.BlockSpec((1, tm, GATE_COLS), lambda b, s: (b, s, 0)),
        ],
        out_shape=[
            jax.ShapeDtypeStruct((B, S, FOX_WIDTH), BF16),
            jax.ShapeDtypeStruct((B, S, 4 * MLSTM_WIDTH), BF16),
            jax.ShapeDtypeStruct((B, 2 * FOX_WIDTH, S), BF16),
            jax.ShapeDtypeStruct((B, S, GATE_COLS), F32),
            jax.ShapeDtypeStruct((B, GATE_COLS, S), F32),
            jax.ShapeDtypeStruct((B, S, GATE_COLS), BF16),
        ],
        scratch_shapes=[pltpu.VMEM((1, GATE_COLS), F32)],
        compiler_params=pltpu.CompilerParams(
            dimension_semantics=("arbitrary", "arbitrary"), vmem_limit_bytes=VMEM_LIMIT),
        name="inproj",
    )(x, g_mix, w_nat, w_t, bias, tri, place)


def _fox_kernel(qt_ref, vt_ref, k_ref, caug_ref, gain_ref, o_ref, sbuf_ref, *, tq, tk):
    j = pl.program_id(1)
    qi = pl.program_id(2)
    q_start = qi * tq
    n_blocks = q_start // tk + 1
    qt = qt_ref[0]
    row = lax.broadcasted_iota(jnp.int32, (2 * FOX_HEAD_DIM, tq), 0)
    rhs = []
    for r in range(2):
        head = 2 * j + r
        qm = jnp.where(row >= FOX_HEAD_DIM * r, jnp.where(row < FOX_HEAD_DIM * (r + 1), qt, 0), 0)
        sel = jnp.where(row >= 3 * head, jnp.where(row < 3 * head + 3, -1.0, 0.0), 0.0).astype(BF16)
        rhs.append(jnp.concatenate([qm.astype(BF16), sel], axis=0))

    def logits(u, slot, masked):
        ks = pl.multiple_of(u * tk, tk)
        lhs = jnp.concatenate([k_ref[0, pl.ds(ks, tk), :], caug_ref[0, pl.ds(ks, tk), :]], axis=1)
        mcs = []
        for r in range(2):
            st = jnp.dot(lhs, rhs[r], preferred_element_type=F32)
            if masked:
                kpos = lax.broadcasted_iota(jnp.int32, (tk, tq), 0) + (ks - q_start)
                qpos = lax.broadcasted_iota(jnp.int32, (tk, tq), 1)
                st = jnp.where(kpos <= qpos, st, NEG_BIG)
            sbuf_ref[slot, r] = st
            mcs.append(jnp.max(st, axis=0, keepdims=True))
        return tuple(mcs)

    ones_rows = jnp.ones((ACC_ROWS - FOX_HEAD_DIM, tk), BF16)

    def absorb(u, slot, mcs, stats):
        ks = pl.multiple_of(u * tk, tk)
        new = []
        for r in range(2):
            m, acc = stats[r]
            mn = jnp.maximum(m, mcs[r])
            alpha = jnp.exp2(m - mn)
            p = jnp.exp2(sbuf_ref[slot, r] - mn)
            vt = vt_ref[0, FOX_HEAD_DIM * r:FOX_HEAD_DIM * (r + 1), pl.ds(ks, tk)]
            vt_aug = jnp.concatenate([vt, ones_rows], axis=0)
            acc = alpha * acc + jnp.dot(vt_aug, p.astype(BF16), preferred_element_type=F32)
            new.append((mn, acc))
        return tuple(new)

    stats = tuple((jnp.full((1, tq), NEG_BIG, F32), jnp.zeros((ACC_ROWS, tq), F32)) for _ in range(2))
    last = n_blocks - 1
    mcs = logits(last, 0, True)

    def body(i, carry):
        prev, mcs, stats = carry
        slot = lax.rem(i, 2)
        stats = absorb(prev, slot, mcs, stats)
        mcs = logits(i, 1 - slot, False)
        return i, mcs, stats

    prev, mcs, stats = lax.fori_loop(0, last, body, (last, mcs, stats))
    stats = absorb(prev, lax.rem(last, 2), mcs, stats)

    outs = []
    for r in range(2):
        _, acc = stats[r]
        o = acc[:FOX_HEAD_DIM] / acc[FOX_HEAD_DIM:FOX_HEAD_DIM + 1]
        ms = jnp.mean(o * o, axis=0, keepdims=True)
        outs.append(o * lax.rsqrt(ms + EPS))
    y = jnp.concatenate(outs, axis=0) * gain_ref[...]
    o_ref[0] = y.T.astype(BF16)


def _fox(qvt, fk, caug, gain_b, *, tq, tk):
    B, S, _ = fk.shape
    pair = 2 * FOX_HEAD_DIM
    n_pairs = FOX_HEADS // 2
    return pl.pallas_call(
        functools.partial(_fox_kernel, tq=tq, tk=tk),
        grid=(B, n_pairs, S // tq),
        in_specs=[
            pl.BlockSpec((1, pair, tq), lambda b, j, q: (b, j, q)),
            pl.BlockSpec((1, pair, S), lambda b, j, q: (b, n_pairs + j, 0)),
            pl.BlockSpec((1, S, pair), lambda b, j, q: (b, 0, j)),
            pl.BlockSpec((1, S, GATE_COLS), lambda b, j, q: (b, 0, 0)),
            pl.BlockSpec((pair, tq), lambda b, j, q: (j, 0)),
        ],
        out_specs=pl.BlockSpec((1, tq, pair), lambda b, j, q: (b, q, j)),
        out_shape=jax.ShapeDtypeStruct((B, S, FOX_WIDTH), BF16),
        scratch_shapes=[pltpu.VMEM((2, 2, tk, tq), F32)],
        compiler_params=pltpu.CompilerParams(
            dimension_semantics=("arbitrary", "arbitrary", "arbitrary"), vmem_limit_bytes=VMEM_LIMIT),
        name="fox",
    )(qvt, qvt, fk, caug, gain_b)


def _mlstm_kernel(mz_ref, gm_ref, gmt_ref, wconv_ref, gain_ref, o_ref,
                  ubuf_ref, state_ref, mstate_ref, *, tm):
    d = MLSTM_HEAD_DIM
    L = CHUNK
    W = MLSTM_WIDTH
    pad = 8

    @pl.when(pl.program_id(1) == 0)
    def _():
        ubuf_ref[0:pad, :] = jnp.zeros((pad, 2 * W), F32)
        state_ref[...] = jnp.zeros_like(state_ref)
        mstate_ref[...] = jnp.zeros_like(mstate_ref)

    ubuf_ref[pad:pad + tm, :] = mz_ref[0, :, :2 * W].astype(F32)
    conv = ubuf_ref[pad:pad + tm, :] * wconv_ref[CONV_WIDTH - 1:CONV_WIDTH, :]
    for jj in range(CONV_WIDTH - 1):
        shift = CONV_WIDTH - 1 - jj
        conv = conv + ubuf_ref[pad - shift:pad - shift + tm, :] * wconv_ref[jj:jj + 1, :]
    ubuf_ref[0:pad, :] = ubuf_ref[tm:tm + pad, :]
    qk = conv * _sigmoid(conv)
    q_all = qk[:, :W].astype(BF16)
    k_all = qk[:, W:] * (d ** -0.5)

    trow = lax.broadcasted_iota(jnp.int32, (L, L), 0)
    scol = lax.broadcasted_iota(jnp.int32, (L, L), 1)
    causal = scol <= trow
    lane = lax.broadcasted_iota(jnp.int32, (L, LANES), 1)
    ones_col = jnp.where(lane == 0, 1.0, 0.0).astype(BF16)
    nt = (((1,), (1,)), ((), ()))

    for c in range(tm // L):
        rs = slice(c * L, (c + 1) * L)
        for hd in range(MLSTM_HEADS):
            cs_ = slice(hd * d, (hd + 1) * d)
            q = q_all[rs, cs_]
            kf = k_all[rs, cs_]
            k = kf.astype(BF16)
            v = mz_ref[0, rs, 2 * W + hd * d:2 * W + (hd + 1) * d]
            v_aug = jnp.concatenate([v, ones_col], axis=1)
            i_col = gm_ref[0, rs, COL_ML_I + hd:COL_ML_I + hd + 1]
            b_col = gm_ref[0, rs, COL_ML_F + hd:COL_ML_F + hd + 1]
            i_row = gmt_ref[0, COL_ML_I + hd:COL_ML_I + hd + 1, rs]
            b_row = gmt_ref[0, COL_ML_F + hd:COL_ML_F + hd + 1, rs]
            m_prev = mstate_ref[hd]
            m_sc = m_prev[:, :1]
            c_aug = state_ref[hd]

            dm = jnp.where(causal, b_col - (b_row - i_row), NEG_BIG)
            inter = b_col + m_sc
            mt = jnp.maximum(inter, jnp.max(dm, axis=-1, keepdims=True))
            w_inter = jnp.exp(inter - mt)
            w_intra = jnp.exp(dm - mt)
            sqk = lax.dot_general(q, k, nt, preferred_element_type=F32) * w_intra
            nd = (w_inter * jnp.dot(q, c_aug.astype(BF16), preferred_element_type=F32)
                  + jnp.dot(sqk.astype(BF16), v_aug, preferred_element_type=F32))
            den = nd[:, d:d + 1]
            hcur = nd[:, :d] / jnp.maximum(jnp.abs(den), jnp.exp(-mt))

            b_last = b_col[L - 1:L, :]
            gcol = b_last - b_col + i_col
            m_new = jnp.maximum(b_last + m_sc, jnp.max(gcol, axis=0, keepdims=True))
            decay = jnp.exp(b_last + m_sc - m_new)
            wk = jnp.exp(gcol - m_new) * kf
            upd = jnp.dot(wk.T.astype(BF16), v_aug, preferred_element_type=F32)
            state_ref[hd] = decay * c_aug + upd
            mstate_ref[hd] = jnp.broadcast_to(m_new, (1, LANES))

            ms = jnp.mean(hcur * hcur, axis=-1, keepdims=True)
            og = mz_ref[0, rs, 3 * W + hd * d:3 * W + (hd + 1) * d].astype(F32)
            y = hcur * lax.rsqrt(ms + EPS) * gain_ref[:, cs_] * _sigmoid(og)
            o_ref[0, rs, cs_] = y.astype(BF16)


def _mlstm(mz, gm, gmt, w_conv, gain, *, tm):
    B, S, _ = mz.shape
    W = MLSTM_WIDTH
    const = lambda *_: (0, 0)
    return pl.pallas_call(
        functools.partial(_mlstm_kernel, tm=tm),
        grid=(B, S // tm),
        in_specs=[
            pl.BlockSpec((1, tm, 4 * W), lambda b, s: (b, s, 0)),
            pl.BlockSpec((1, tm, GATE_COLS), lambda b, s: (b, s, 0)),
            pl.BlockSpec((1, GATE_COLS, tm), lambda b, s: (b, 0, s)),
            pl.BlockSpec((CONV_WIDTH, 2 * W), const),
            pl.BlockSpec((1, W), const),
        ],
        out_specs=pl.BlockSpec((1, tm, W), lambda b, s: (b, s, 0)),
        out_shape=jax.ShapeDtypeStruct((B, S, W), BF16),
        scratch_shapes=[
            pltpu.VMEM((tm + 8, 2 * W), F32),
            pltpu.VMEM((MLSTM_HEADS, MLSTM_HEAD_DIM, 2 * MLSTM_HEAD_DIM), F32),
            pltpu.VMEM((MLSTM_HEADS, 1, LANES), F32),
        ],
        compiler_params=pltpu.CompilerParams(
            dimension_semantics=("arbitrary", "arbitrary"), vmem_limit_bytes=VMEM_LIMIT),
        name="mlstm",
    )(mz, gm, gmt, w_conv, gain)


def _rms(x, g):
    ms = jnp.mean(x * x, axis=-1, keepdims=True)
    return x * lax.rsqrt(ms + EPS) * g


def _tail_kernel(x_ref, fox_ref, ml_ref, p_ref, wo_ref, wup_ref, wdn_ref, wpg_ref, wple_ref,
                 gmlp_ref, gple_ref, gfin_ref, o_ref, *, ff_chunk):
    mix = jnp.concatenate([fox_ref[...], ml_ref[...]], axis=1)
    x1 = x_ref[...] + jnp.dot(mix, wo_ref[...], preferred_element_type=F32)
    hm = _rms(x1, gmlp_ref[...]).astype(BF16)
    x2 = x1
    d_ff = wup_ref.shape[1]
    for f in range(d_ff // ff_chunk):
        fs = slice(f * ff_chunk, (f + 1) * ff_chunk)
        u = jnp.maximum(jnp.dot(hm, wup_ref[:, fs], preferred_element_type=F32), 0.0)
        x2 = x2 + jnp.dot((u * u).astype(BF16), wdn_ref[fs, :], preferred_element_type=F32)
    hg = _rms(x2, gple_ref[...]).astype(BF16)
    gate = _sigmoid(jnp.dot(hg, wpg_ref[...], preferred_element_type=F32))
    pe = jnp.dot(p_ref[...].astype(BF16), wple_ref[...], preferred_element_type=F32)
    x3 = x2 + gate * pe
    o_ref[...] = _rms(x3, gfin_ref[...])


def _tail(x2d, fox2d, ml2d, p2d, wo, wup, wdn, wpg, wple, gmlp, gple, gfin, *, tm):
    N, D = x2d.shape
    const = lambda i: (0, 0)
    once = pl.Buffered(1)
    wspec = lambda a: pl.BlockSpec(a.shape, const, pipeline_mode=once)
    return pl.pallas_call(
        functools.partial(_tail_kernel, ff_chunk=1024),
        grid=(N // tm,),
        in_specs=[
            pl.BlockSpec((tm, D), lambda i: (i, 0)),
            pl.BlockSpec((tm, fox2d.shape[1]), lambda i: (i, 0)),
            pl.BlockSpec((tm, ml2d.shape[1]), lambda i: (i, 0)),
            pl.BlockSpec((tm, p2d.shape[1]), lambda i: (i, 0)),
            wspec(wo), wspec(wup), wspec(wdn), wspec(wpg), wspec(wple),
            wspec(gmlp), wspec(gple), wspec(gfin),
        ],
        out_specs=pl.BlockSpec((tm, D), lambda i: (i, 0)),
        out_shape=jax.ShapeDtypeStruct((N, D), F32),
        compiler_params=pltpu.CompilerParams(
            dimension_semantics=("arbitrary",), vmem_limit_bytes=VMEM_LIMIT),
        name="tail",
    )(x2d, fox2d, ml2d, p2d, wo, wup, wdn, wpg, wple, gmlp, gple, gfin)


def _layer(x, p_i, w_in, b_fox_f, b_mlstm_i, b_mlstm_f, w_conv, g_mix, g_fox_out, g_mlstm_out,
           w_out, g_mlp, w_up, w_down, w_ple, g_ple, w_ple_gate, g_final):
    B, S, D = x.shape
    fw, mw = FOX_WIDTH, MLSTM_WIDTH
    o = 0
    cols = {}
    for name, size in (("fq", fw), ("fk", fw), ("fv", fw), ("ff", FOX_HEADS), ("mq", mw), ("mk", mw),
                       ("mv", mw), ("mi", MLSTM_HEADS), ("mf", MLSTM_HEADS), ("mo", mw)):
        cols[name] = w_in[:, o:o + size]
        o += size
    gate_w = jnp.concatenate([cols["ff"], cols["mi"], cols["mf"]], axis=1)
    gate_w = jnp.pad(gate_w, ((0, 0), (0, GATE_COLS - gate_w.shape[1])))
    w_nat = jnp.concatenate([cols["fk"], cols["mq"], cols["mk"], cols["mv"], cols["mo"], gate_w], axis=1).astype(BF16)
    w_t = jnp.concatenate([cols["fq"], cols["fv"]], axis=1).T.astype(BF16)
    bias = jnp.concatenate([b_fox_f, b_mlstm_i, b_mlstm_f])
    bias = jnp.pad(bias, (0, GATE_COLS - bias.shape[0]))[None, :].astype(F32)

    fk, mz, qvt, gm, gmt, caug = _inproj(x, g_mix[None, :], w_nat, w_t, bias, tm=512)

    tq = 512
    gain_b = jnp.broadcast_to(g_fox_out.astype(F32)[:, None], (fw, tq))
    fox = _fox(qvt, fk, caug, gain_b, tq=tq, tk=512)
    ml = _mlstm(mz, gm, gmt, w_conv.astype(F32), g_mlstm_out[None, :].astype(F32), tm=512)

    out = _tail(x.reshape(B * S, D), fox.reshape(B * S, fw), ml.reshape(B * S, mw),
                p_i.reshape(B * S, p_i.shape[-1]),
                w_out.astype(BF16), w_up.astype(BF16), w_down.astype(BF16),
                w_ple_gate.astype(BF16), w_ple.astype(BF16),
                g_mlp[None, :], g_ple[None, :], g_final[None, :], tm=512)
    return out.reshape(B, S, D)


def kernel(x, p, w_in, b_fox_f, b_mlstm_i, b_mlstm_f, w_conv, g_mix, g_fox_out, g_mlstm_out, w_out,
           g_mlp, w_up, w_down, w_ple, g_ple, w_ple_gate, g_final):
    assert w_in.shape[0] == 1, "single-layer kernel: the final RMSNorm is fused into the layer"
    return _layer(x, p[0], w_in[0], b_fox_f[0], b_mlstm_i[0], b_mlstm_f[0], w_conv[0], g_mix[0],
                  g_fox_out[0], g_mlstm_out[0], w_out[0], g_mlp[0], w_up[0], w_down[0], w_ple[0],
                  g_ple[0], w_ple_gate[0], g_final)
```

```python
import functools
import math

import jax
import jax.numpy as jnp
from jax import lax
from jax.experimental import pallas as pl
from jax.experimental.pallas import tpu as pltpu

F32 = jnp.float32
BF16 = jnp.bfloat16

EPS = 1e-6
FOX_HEADS = 8
FOX_HEAD_DIM = 64
FOX_WIDTH = FOX_HEADS * FOX_HEAD_DIM
ACC_ROWS = FOX_HEAD_DIM + 16
MLSTM_HEADS = 4
MLSTM_HEAD_DIM = 128
MLSTM_WIDTH = MLSTM_HEADS * MLSTM_HEAD_DIM
CONV_WIDTH = 4
CHUNK = 128
LANES = 128
GATE_COLS = LANES
LOG2E = 1.4426950408889634
NEG_BIG = -1e30
VMEM_LIMIT = 56 * 1024 * 1024

COL_FOX_F = 0
COL_ML_I = FOX_HEADS
COL_ML_F = FOX_HEADS + MLSTM_HEADS


def _split3(x):
    hi = x.astype(BF16)
    r1 = x - hi.astype(F32)
    mid = r1.astype(BF16)
    lo = (r1 - mid.astype(F32)).astype(BF16)
    return hi, mid, lo


def _log_sigmoid(x):
    return jnp.minimum(x, 0.0) - jnp.log1p(jnp.exp(-jnp.abs(x)))


def _sigmoid(x):
    return 0.5 * jnp.tanh(0.5 * x) + 0.5


def _inproj_kernel(x_ref, g_ref, wn_ref, wt_ref, bias_ref, tri_ref, place_ref,
                   fk_ref, mz_ref, qvt_ref, gm_ref, gmt_ref, caug_ref, carry_ref, *, tm, qscale):
    @pl.when(pl.program_id(1) == 0)
    def _():
        carry_ref[...] = jnp.zeros_like(carry_ref)

    x = x_ref[0]
    ms = jnp.mean(x * x, axis=-1, keepdims=True)
    h = (x * lax.rsqrt(ms + EPS) * g_ref[...]).astype(BF16)

    z = jnp.dot(h, wn_ref[...], preferred_element_type=F32)
    fk_ref[0] = z[:, :FOX_WIDTH].astype(BF16)
    mz_ref[0] = z[:, FOX_WIDTH:FOX_WIDTH + 4 * MLSTM_WIDTH].astype(BF16)

    nt = (((1,), (1,)), ((), ()))
    qt = lax.dot_general(wt_ref[:FOX_WIDTH, :], h, nt, preferred_element_type=F32)
    qvt_ref[0, :FOX_WIDTH, :] = (qt * qscale).astype(BF16)
    vt = lax.dot_general(wt_ref[FOX_WIDTH:, :], h, nt, preferred_element_type=F32)
    qvt_ref[0, FOX_WIDTH:, :] = vt.astype(BF16)

    g = z[:, FOX_WIDTH + 4 * MLSTM_WIDTH:] + bias_ref[...]
    logf = _log_sigmoid(g)
    hi, mid, lo = _split3(logf)
    cs3 = jnp.dot(tri_ref[...], jnp.concatenate([hi, mid, lo], axis=1), preferred_element_type=F32)
    cs = cs3[:, :GATE_COLS] + cs3[:, GATE_COLS:2 * GATE_COLS] + cs3[:, 2 * GATE_COLS:]
    pieces = []
    for c in range(tm // CHUNK):
        blk = cs[c * CHUNK:(c + 1) * CHUNK]
        if c > 0:
            blk = blk - cs[c * CHUNK - 1:c * CHUNK]
        pieces.append(blk)
    bloc = jnp.concatenate(pieces, axis=0)
    cfox = (carry_ref[...] + cs) * LOG2E
    carry_ref[...] = carry_ref[...] + cs[tm - 1:tm]

    col = lax.broadcasted_iota(jnp.int32, (tm, GATE_COLS), 1)
    gm = jnp.where(col < COL_ML_I, cfox, jnp.where(col < COL_ML_F, g, bloc) * LOG2E)
    gm_ref[0] = gm
    gmt_ref[0] = gm.T
    chi, cmid, clo = _split3(cfox)
    caug = jnp.dot(jnp.concatenate([chi, cmid, clo], axis=1), place_ref[...], preferred_element_type=F32)
    caug_ref[0] = caug.astype(BF16)


def _inproj(x, g_mix, w_nat, w_t, bias, *, tm):
    B, S, D = x.shape
    n_nat = w_nat.shape[1]
    qscale = FOX_HEAD_DIM ** -0.5 * LOG2E
    tri = (jnp.arange(tm)[:, None] >= jnp.arange(tm)[None, :]).astype(BF16)
    rows = jnp.arange(3 * GATE_COLS)
    piece, head = rows // GATE_COLS, rows % GATE_COLS
    place = ((head[:, None] < FOX_HEADS) & (jnp.arange(GATE_COLS)[None, :] == (3 * head + piece)[:, None])).astype(BF16)
    const = lambda *_: (0, 0)
    return pl.pallas_call(
        functools.partial(_inproj_kernel, tm=tm, qscale=qscale),
        grid=(B, S // tm),
        in_specs=[
            pl.BlockSpec((1, tm, D), lambda b, s: (b, s, 0)),
            pl.BlockSpec((1, D), const),
            pl.BlockSpec((D, n_nat), const),
            pl.BlockSpec((2 * FOX_WIDTH, D), const),
            pl.BlockSpec((1, GATE_COLS), const),
            pl.BlockSpec((tm, tm), const),
            pl.BlockSpec((3 * GATE_COLS, GATE_COLS), const),
        ],
        out_specs=[
            pl.BlockSpec((1, tm, FOX_WIDTH), lambda b, s: (b, s, 0)),
            pl.BlockSpec((1, tm, 4 * MLSTM_WIDTH), lambda b, s: (b, s, 0)),
            pl.BlockSpec((1, 2 * FOX_WIDTH, tm), lambda b, s: (b, 0, s)),
            pl.BlockSpec((1, tm, GATE_COLS), lambda b, s: (b, s, 0)),
            pl.BlockSpec((1, GATE_COLS, tm), lambda b, s: (b, 0, s)),
            pl.BlockSpec((1, tm, GATE_COLS), lambda b, s: (b, s, 0)),
        ],
        out_shape=[
            jax.ShapeDtypeStruct((B, S, FOX_WIDTH), BF16),
            jax.ShapeDtypeStruct((B, S, 4 * MLSTM_WIDTH), BF16),
            jax.ShapeDtypeStruct((B, 2 * FOX_WIDTH, S), BF16),
            jax.ShapeDtypeStruct((B, S, GATE_COLS), F32),
            jax.ShapeDtypeStruct((B, GATE_COLS, S), F32),
            jax.ShapeDtypeStruct((B, S, GATE_COLS), BF16),
        ],
        scratch_shapes=[pltpu.VMEM((1, GATE_COLS), F32)],
        compiler_params=pltpu.CompilerParams(
            dimension_semantics=("arbitrary", "arbitrary"), vmem_limit_bytes=VMEM_LIMIT),
        name="inproj",
    )(x, g_mix, w_nat, w_t, bias, tri, place)


def _fox_kernel(qt_ref, vt_ref, k_ref, caug_ref, gain_ref, o_ref, qm_ref, sbuf_ref, pbuf_ref, *, tb, nq):
    j = pl.program_id(1)
    n_pairs = nq * (nq + 1) // 2
    hd2 = 2 * FOX_HEAD_DIM

    row = lax.broadcasted_iota(jnp.int32, (hd2, tb), 0)
    sel = []
    for r in range(2):
        head = 2 * j + r
        for qi in range(nq):
            qt = qt_ref[0, :, qi * tb:(qi + 1) * tb]
            qm_ref[r, :, qi * tb:(qi + 1) * tb] = jnp.where(
                row >= FOX_HEAD_DIM * r, jnp.where(row < FOX_HEAD_DIM * (r + 1), qt, 0), 0).astype(BF16)
        sel.append(jnp.where(row >= 3 * head, jnp.where(row < 3 * head + 3, -1.0, 0.0), 0.0).astype(BF16))
    ones_rows = jnp.ones((ACC_ROWS - FOX_HEAD_DIM, tb), BF16)

    def start_of(idx):
        return idx * tb if isinstance(idx, int) else pl.multiple_of(idx * tb, tb)

    def stage1(qi, u, slot, masked):
        ks, qs = start_of(u), start_of(qi)
        lhs = jnp.concatenate([k_ref[0, pl.ds(ks, tb), :], caug_ref[0, pl.ds(ks, tb), :]], axis=1)
        mcs = []
        for r in range(2):
            rhs = jnp.concatenate([qm_ref[r, :, pl.ds(qs, tb)], sel[r]], axis=0)
            st = jnp.dot(lhs, rhs, preferred_element_type=F32)
            if masked:
                kpos = lax.broadcasted_iota(jnp.int32, (tb, tb), 0)
                qpos = lax.broadcasted_iota(jnp.int32, (tb, tb), 1)
                st = jnp.where(kpos <= qpos, st, NEG_BIG)
            sbuf_ref[slot, r] = st
            mcs.append(jnp.max(st, axis=0, keepdims=True))
        return tuple(mcs)

    def stage2(slot, mcs, ms, first):
        new_m, alphas = [], []
        for r in range(2):
            m_prev = jnp.where(first, NEG_BIG, ms[r])
            mn = jnp.maximum(m_prev, mcs[r])
            alphas.append(jnp.exp2(m_prev - mn))
            pbuf_ref[slot, r] = jnp.exp2(sbuf_ref[slot, r] - mn).astype(BF16)
            new_m.append(mn)
        return tuple(new_m), tuple(alphas)

    def stage3(u, slot, alphas, accs):
        ks = start_of(u)
        out = []
        for r in range(2):
            vt = vt_ref[0, FOX_HEAD_DIM * r:FOX_HEAD_DIM * (r + 1), pl.ds(ks, tb)]
            vt_aug = jnp.concatenate([vt, ones_rows], axis=0)
            out.append(alphas[r] * accs[r] + jnp.dot(vt_aug, pbuf_ref[slot, r], preferred_element_type=F32))
        return tuple(out)

    def finalize(qi, accs):
        outs = []
        for r in range(2):
            o = accs[r][:FOX_HEAD_DIM] / accs[r][FOX_HEAD_DIM:FOX_HEAD_DIM + 1]
            ms = jnp.mean(o * o, axis=0, keepdims=True)
            outs.append(o * lax.rsqrt(ms + EPS))
        y = jnp.concatenate(outs, axis=0) * gain_ref[...]
        o_ref[0, pl.ds(start_of(qi), tb), :] = y.T.astype(BF16)

    ms = tuple(jnp.full((1, tb), NEG_BIG, F32) for _ in range(2))
    accs = tuple(jnp.zeros((ACC_ROWS, tb), F32) for _ in range(2))
    mcs = stage1(0, 0, 0, True)
    ms, alphas = stage2(0, mcs, ms, True)
    mcs = stage1(1, 0, 1, False)

    def tick(carry, slot, masked):
        (q1, u1), (q2, u2), (q3, u3), mcs, ms, alphas, accs = carry
        accs = stage3(u3, slot, alphas, accs)
        ms, alphas = stage2(1 - slot, mcs, ms, u2 == 0)
        mcs = stage1(q1, u1, slot, masked)
        return (q1, u1), (q2, u2), (q3, u3), mcs, ms, alphas, accs

    def one_tick(carry, slot):
        (q1, u1), (q2, u2), (q3, u3) = carry[:3]
        carry = lax.cond(u1 == q1, lambda c: tick(c, slot, True), lambda c: tick(c, slot, False), carry)
        accs = carry[6]

        @pl.when(u3 == q3)
        def _():
            finalize(q3, accs)

        wrap = u1 == q1
        nxt = (jnp.where(wrap, q1 + 1, q1), jnp.where(wrap, 0, u1 + 1))
        return (nxt, (q1, u1), (q2, u2)) + tuple(carry[3:])

    assert n_pairs % 2 == 0
    i32 = lambda x: jnp.asarray(x, jnp.int32)
    carry = ((i32(1), i32(1)), (i32(1), i32(0)), (i32(0), i32(0)), mcs, ms, alphas, accs)
    carry = lax.fori_loop(1, n_pairs // 2, lambda i, c: one_tick(one_tick(c, 0), 1), carry)
    mcs, ms, alphas, accs = carry[3:]

    accs = stage3(nq - 2, n_pairs % 2, alphas, accs)
    ms, alphas = stage2((n_pairs - 1) % 2, mcs, ms, False)
    accs = stage3(nq - 1, (n_pairs - 1) % 2, alphas, accs)
    finalize(nq - 1, accs)


def _fox(qvt, fk, caug, gain_b, *, tb):
    B, S, _ = fk.shape
    pair = 2 * FOX_HEAD_DIM
    n_pairs = FOX_HEADS // 2
    nq = S // tb
    assert nq >= 2
    return pl.pallas_call(
        functools.partial(_fox_kernel, tb=tb, nq=nq),
        grid=(B, n_pairs),
        in_specs=[
            pl.BlockSpec((1, pair, S), lambda b, j: (b, j, 0)),
            pl.BlockSpec((1, pair, S), lambda b, j: (b, n_pairs + j, 0)),
            pl.BlockSpec((1, S, pair), lambda b, j: (b, 0, j)),
            pl.BlockSpec((1, S, GATE_COLS), lambda b, j: (b, 0, 0)),
            pl.BlockSpec((pair, tb), lambda b, j: (j, 0)),
        ],
        out_specs=pl.BlockSpec((1, S, pair), lambda b, j: (b, 0, j)),
        out_shape=jax.ShapeDtypeStruct((B, S, FOX_WIDTH), BF16),
        scratch_shapes=[
            pltpu.VMEM((2, pair, S), BF16),
            pltpu.VMEM((2, 2, tb, tb), F32),
            pltpu.VMEM((2, 2, tb, tb), BF16),
        ],
        compiler_params=pltpu.CompilerParams(
            dimension_semantics=("arbitrary", "arbitrary"), vmem_limit_bytes=VMEM_LIMIT),
        name="fox",
    )(qvt, qvt, fk, caug, gain_b)


def _mlstm_kernel(mz_ref, gm_ref, gmt_ref, wconv_ref, gain_ref, o_ref,
                  ubuf_ref, state_ref, mstate_ref, *, tm):
    d = MLSTM_HEAD_DIM
    L = CHUNK
    W = MLSTM_WIDTH
    pad = 8

    @pl.when(pl.program_id(1) == 0)
    def _():
        ubuf_ref[0:pad, :] = jnp.zeros((pad, 2 * W), F32)
        state_ref[...] = jnp.zeros_like(state_ref)
        mstate_ref[...] = jnp.zeros_like(mstate_ref)

    ubuf_ref[pad:pad + tm, :] = mz_ref[0, :, :2 * W].astype(F32)
    conv = ubuf_ref[pad:pad + tm, :] * wconv_ref[CONV_WIDTH - 1:CONV_WIDTH, :]
    for jj in range(CONV_WIDTH - 1):
        shift = CONV_WIDTH - 1 - jj
        conv = conv + ubuf_ref[pad - shift:pad - shift + tm, :] * wconv_ref[jj:jj + 1, :]
    ubuf_ref[0:pad, :] = ubuf_ref[tm:tm + pad, :]
    qk = conv + conv * jnp.tanh(conv)
    q_all = qk[:, :W]
    k_all = qk[:, W:] * (d ** -0.5)

    trow = lax.broadcasted_iota(jnp.int32, (L, L), 0)
    scol = lax.broadcasted_iota(jnp.int32, (L, L), 1)
    causal = scol <= trow
    ones_blk = jnp.ones((L, LANES), BF16)
    nt = (((1,), (1,)), ((), ()))

    c_state = [state_ref[hd] for hd in range(MLSTM_HEADS)]
    m_state = [mstate_ref[hd] for hd in range(MLSTM_HEADS)]
    for c in range(tm // L):
        rs = slice(c * L, (c + 1) * L)
        for hd in range(MLSTM_HEADS):
            cs_ = slice(hd * d, (hd + 1) * d)
            qf = q_all[rs, cs_]
            kf = k_all[rs, cs_]
            v = mz_ref[0, rs, 2 * W + hd * d:2 * W + (hd + 1) * d]
            v_aug = jnp.concatenate([v, ones_blk], axis=1)
            i_col = gm_ref[0, rs, COL_ML_I + hd:COL_ML_I + hd + 1]
            b_col = gm_ref[0, rs, COL_ML_F + hd:COL_ML_F + hd + 1]
            i_row = gmt_ref[0, COL_ML_I + hd:COL_ML_I + hd + 1, rs]
            b_row = gmt_ref[0, COL_ML_F + hd:COL_ML_F + hd + 1, rs]
            emat = jnp.where(causal, i_row - b_row, NEG_BIG)
            cmb = jnp.broadcast_to(jnp.max(emat, axis=-1, keepdims=True), (L, LANES))
            ecb = jnp.broadcast_to(i_col - b_col, (L, LANES))
            bb = jnp.broadcast_to(b_col, (L, LANES))
            s = lax.dot_general(qf.astype(BF16), kf.astype(BF16), nt, preferred_element_type=F32)

            m = m_state[hd]
            c_aug = c_state[hd]
            mx = jnp.maximum(m, cmb)
            mx_last = mx[L - 1:L, :]
            w_inter = jnp.exp2(m - mx)
            w_intra = jnp.exp2(emat - mx)
            lhs = jnp.concatenate([(w_inter * qf).astype(BF16), (s * w_intra).astype(BF16)], axis=1)
            rhs = jnp.concatenate([c_aug.astype(BF16), v_aug], axis=0)
            nd = jnp.dot(lhs, rhs, preferred_element_type=F32)
            hcur = nd[:, :d] / jnp.maximum(jnp.abs(nd[:, d:]), jnp.exp2(-(bb + mx)))

            wk = jnp.exp2(ecb - mx_last) * kf
            upd = jnp.dot(wk.T.astype(BF16), v_aug, preferred_element_type=F32)
            decay = jnp.exp2(m - mx_last)
            c_state[hd] = jnp.concatenate([decay, decay], axis=1) * c_aug + upd
            m_state[hd] = bb[L - 1:L, :] + mx_last

            ms = jnp.mean(hcur * hcur, axis=-1, keepdims=True)
            og = mz_ref[0, rs, 3 * W + hd * d:3 * W + (hd + 1) * d].astype(F32)
            y = hcur * lax.rsqrt(ms + EPS) * gain_ref[:, cs_]
            o_ref[0, rs, cs_] = (y * jnp.tanh(og) + y).astype(BF16)
    for hd in range(MLSTM_HEADS):
        state_ref[hd] = c_state[hd]
        mstate_ref[hd] = m_state[hd]


def _mlstm(mz, gm, gmt, w_conv, gain, *, tm):
    B, S, _ = mz.shape
    W = MLSTM_WIDTH
    const = lambda *_: (0, 0)
    return pl.pallas_call(
        functools.partial(_mlstm_kernel, tm=tm),
        grid=(B, S // tm),
        in_specs=[
            pl.BlockSpec((1, tm, 4 * W), lambda b, s: (b, s, 0)),
            pl.BlockSpec((1, tm, GATE_COLS), lambda b, s: (b, s, 0)),
            pl.BlockSpec((1, GATE_COLS, tm), lambda b, s: (b, 0, s)),
            pl.BlockSpec((CONV_WIDTH, 2 * W), const),
            pl.BlockSpec((1, W), const),
        ],
        out_specs=pl.BlockSpec((1, tm, W), lambda b, s: (b, s, 0)),
        out_shape=jax.ShapeDtypeStruct((B, S, W), BF16),
        scratch_shapes=[
            pltpu.VMEM((tm + 8, 2 * W), F32),
            pltpu.VMEM((MLSTM_HEADS, MLSTM_HEAD_DIM, 2 * MLSTM_HEAD_DIM), F32),
            pltpu.VMEM((MLSTM_HEADS, 1, LANES), F32),
        ],
        compiler_params=pltpu.CompilerParams(
            dimension_semantics=("arbitrary", "arbitrary"), vmem_limit_bytes=VMEM_LIMIT),
        name="mlstm",
    )(mz, gm, gmt, w_conv, gain)


def _rms(x, g):
    ms = jnp.mean(x * x, axis=-1, keepdims=True)
    return x * lax.rsqrt(ms + EPS) * g


def _tail_kernel(x_ref, fox_ref, ml_ref, p_ref, wo_ref, wup_ref, wdn_ref, wpg_ref, wple_ref,
                 gmlp_ref, gple_ref, gfin_ref, o_ref, *, ff_chunk):
    mix = jnp.concatenate([fox_ref[...], ml_ref[...]], axis=1)
    x1 = x_ref[...] + jnp.dot(mix, wo_ref[...], preferred_element_type=F32)
    hm = _rms(x1, gmlp_ref[...]).astype(BF16)
    x2 = x1
    d_ff = wup_ref.shape[1]
    for f in range(d_ff // ff_chunk):
        fs = slice(f * ff_chunk, (f + 1) * ff_chunk)
        u = jnp.maximum(jnp.dot(hm, wup_ref[:, fs], preferred_element_type=F32), 0.0)
        x2 = x2 + jnp.dot((u * u).astype(BF16), wdn_ref[fs, :], preferred_element_type=F32)
    hg = _rms(x2, gple_ref[...]).astype(BF16)
    gate = _sigmoid(jnp.dot(hg, wpg_ref[...], preferred_element_type=F32))
    pe = jnp.dot(p_ref[...].astype(BF16), wple_ref[...], preferred_element_type=F32)
    x3 = x2 + gate * pe
    o_ref[...] = _rms(x3, gfin_ref[...])


def _tail(x2d, fox2d, ml2d, p2d, wo, wup, wdn, wpg, wple, gmlp, gple, gfin, *, tm):
    N, D = x2d.shape
    const = lambda i: (0, 0)
    once = pl.Buffered(1)
    wspec = lambda a: pl.BlockSpec(a.shape, const, pipeline_mode=once)
    return pl.pallas_call(
        functools.partial(_tail_kernel, ff_chunk=1024),
        grid=(N // tm,),
        in_specs=[
            pl.BlockSpec((tm, D), lambda i: (i, 0)),
            pl.BlockSpec((tm, fox2d.shape[1]), lambda i: (i, 0)),
            pl.BlockSpec((tm, ml2d.shape[1]), lambda i: (i, 0)),
            pl.BlockSpec((tm, p2d.shape[1]), lambda i: (i, 0)),
            wspec(wo), wspec(wup), wspec(wdn), wspec(wpg), wspec(wple),
            wspec(gmlp), wspec(gple), wspec(gfin),
        ],
        out_specs=pl.BlockSpec((tm, D), lambda i: (i, 0)),
        out_shape=jax.ShapeDtypeStruct((N, D), F32),
        compiler_params=pltpu.CompilerParams(
            dimension_semantics=("arbitrary",), vmem_limit_bytes=VMEM_LIMIT),
        name="tail",
    )(x2d, fox2d, ml2d, p2d, wo, wup, wdn, wpg, wple, gmlp, gple, gfin)


def _layer(x, p_i, w_in, b_fox_f, b_mlstm_i, b_mlstm_f, w_conv, g_mix, g_fox_out, g_mlstm_out,
           w_out, g_mlp, w_up, w_down, w_ple, g_ple, w_ple_gate, g_final):
    B, S, D = x.shape
    fw, mw = FOX_WIDTH, MLSTM_WIDTH
    o = 0
    cols = {}
    for name, size in (("fq", fw), ("fk", fw), ("fv", fw), ("ff", FOX_HEADS), ("mq", mw), ("mk", mw),
                       ("mv", mw), ("mi", MLSTM_HEADS), ("mf", MLSTM_HEADS), ("mo", mw)):
        cols[name] = w_in[:, o:o + size]
        o += size
    gate_w = jnp.concatenate([cols["ff"], cols["mi"], cols["mf"]], axis=1)
    gate_w = jnp.pad(gate_w, ((0, 0), (0, GATE_COLS - gate_w.shape[1])))
    w_nat = jnp.concatenate([cols["fk"], cols["mq"], cols["mk"], cols["mv"], 0.5 * cols["mo"], gate_w],
                            axis=1).astype(BF16)
    w_t = jnp.concatenate([cols["fq"], cols["fv"]], axis=1).T.astype(BF16)
    bias = jnp.concatenate([b_fox_f, b_mlstm_i, b_mlstm_f])
    bias = jnp.pad(bias, (0, GATE_COLS - bias.shape[0]))[None, :].astype(F32)

    fk, mz, qvt, gm, gmt, caug = _inproj(x, g_mix[None, :], w_nat, w_t, bias, tm=512)

    tb = 512
    gain_b = jnp.broadcast_to(g_fox_out.astype(F32)[:, None], (fw, tb))
    fox = _fox(qvt, fk, caug, gain_b, tb=tb)
    ml = _mlstm(mz, gm, gmt, 0.5 * w_conv.astype(F32), 0.5 * g_mlstm_out[None, :].astype(F32), tm=512)

    out = _tail(x.reshape(B * S, D), fox.reshape(B * S, fw), ml.reshape(B * S, mw),
                p_i.reshape(B * S, p_i.shape[-1]),
                w_out.astype(BF16), w_up.astype(BF16), w_down.astype(BF16),
                w_ple_gate.astype(BF16), w_ple.astype(BF16),
                g_mlp[None, :], g_ple[None, :], g_final[None, :], tm=512)
    return out.reshape(B, S, D)


def kernel(x, p, w_in, b_fox_f, b_mlstm_i, b_mlstm_f, w_conv, g_mix, g_fox_out, g_mlstm_out, w_out,
           g_mlp, w_up, w_down, w_ple, g_ple, w_ple_gate, g_final):
    assert w_in.shape[0] == 1, "single-layer kernel: the final RMSNorm is fused into the layer"
    return _layer(x, p[0], w_in[0], b_fox_f[0], b_mlstm_i[0], b_mlstm_f[0], w_conv[0], g_mix[0],
                  g_fox_out[0], g_mlstm_out[0], w_out[0], g_mlp[0], w_up[0], w_down[0], w_ple[0],
                  g_ple[0], w_ple_gate[0], g_final)
```

```python
import functools
import math

import jax
import jax.numpy as jnp
from jax import lax
from jax.experimental import pallas as pl
from jax.experimental.pallas import tpu as pltpu

F32 = jnp.float32
BF16 = jnp.bfloat16

EPS = 1e-6
FOX_HEADS = 8
FOX_HEAD_DIM = 64
FOX_WIDTH = FOX_HEADS * FOX_HEAD_DIM
ACC_ROWS = FOX_HEAD_DIM + 16
MLSTM_HEADS = 4
MLSTM_HEAD_DIM = 128
MLSTM_WIDTH = MLSTM_HEADS * MLSTM_HEAD_DIM
CONV_WIDTH = 4
CHUNK = 128
LANES = 128
GATE_COLS = LANES
LOG2E = 1.4426950408889634
NEG_BIG = -1e30
VMEM_LIMIT = 56 * 1024 * 1024

COL_FOX_F = 0
COL_ML_I = FOX_HEADS
COL_ML_F = FOX_HEADS + MLSTM_HEADS


def _split3(x):
    hi = x.astype(BF16)
    r1 = x - hi.astype(F32)
    mid = r1.astype(BF16)
    lo = (r1 - mid.astype(F32)).astype(BF16)
    return hi, mid, lo


def _log_sigmoid(x):
    return jnp.minimum(x, 0.0) - jnp.log1p(jnp.exp(-jnp.abs(x)))


def _sigmoid(x):
    return 0.5 * jnp.tanh(0.5 * x) + 0.5


def _inproj_kernel(x_ref, g_ref, wn_ref, wt_ref, bias_ref, tri_ref, place_ref,
                   fk_ref, mz_ref, qvt_ref, gm_ref, gmt_ref, caug_ref, carry_ref, *, tm, qscale):
    @pl.when(pl.program_id(1) == 0)
    def _():
        carry_ref[...] = jnp.zeros_like(carry_ref)

    x = x_ref[0]
    ms = jnp.mean(x * x, axis=-1, keepdims=True)
    h = (x * lax.rsqrt(ms + EPS) * g_ref[...]).astype(BF16)

    z = jnp.dot(h, wn_ref[...], preferred_element_type=F32)
    fk_ref[0] = z[:, :FOX_WIDTH].astype(BF16)
    mz_ref[0] = z[:, FOX_WIDTH:FOX_WIDTH + 4 * MLSTM_WIDTH].astype(BF16)

    nt = (((1,), (1,)), ((), ()))
    qt = lax.dot_general(wt_ref[:FOX_WIDTH, :], h, nt, preferred_element_type=F32)
    qvt_ref[0, :FOX_WIDTH, :] = (qt * qscale).astype(BF16)
    vt = lax.dot_general(wt_ref[FOX_WIDTH:, :], h, nt, preferred_element_type=F32)
    qvt_ref[0, FOX_WIDTH:, :] = vt.astype(BF16)

    g = z[:, FOX_WIDTH + 4 * MLSTM_WIDTH:] + bias_ref[...]
    logf = _log_sigmoid(g)
    hi, mid, lo = _split3(logf)
    cs3 = jnp.dot(tri_ref[...], jnp.concatenate([hi, mid, lo], axis=1), preferred_element_type=F32)
    cs = cs3[:, :GATE_COLS] + cs3[:, GATE_COLS:2 * GATE_COLS] + cs3[:, 2 * GATE_COLS:]
    pieces = []
    for c in range(tm // CHUNK):
        blk = cs[c * CHUNK:(c + 1) * CHUNK]
        if c > 0:
            blk = blk - cs[c * CHUNK - 1:c * CHUNK]
        pieces.append(blk)
    bloc = jnp.concatenate(pieces, axis=0)
    cfox = (carry_ref[...] + cs) * LOG2E
    carry_ref[...] = carry_ref[...] + cs[tm - 1:tm]

    col = lax.broadcasted_iota(jnp.int32, (tm, GATE_COLS), 1)
    gm = jnp.where(col < COL_ML_I, cfox, jnp.where(col < COL_ML_F, g, bloc) * LOG2E)
    gm_ref[0] = gm
    gmt_ref[0] = gm.T
    chi, cmid, clo = _split3(cfox)
    caug = jnp.dot(jnp.concatenate([chi, cmid, clo], axis=1), place_ref[...], preferred_element_type=F32)
    caug_ref[0] = caug.astype(BF16)


def _inproj(x, g_mix, w_nat, w_t, bias, *, tm):
    B, S, D = x.shape
    n_nat = w_nat.shape[1]
    qscale = FOX_HEAD_DIM ** -0.5 * LOG2E
    tri = (jnp.arange(tm)[:, None] >= jnp.arange(tm)[None, :]).astype(BF16)
    rows = jnp.arange(3 * GATE_COLS)
    piece, head = rows // GATE_COLS, rows % GATE_COLS
    place = ((head[:, None] < FOX_HEADS) & (jnp.arange(GATE_COLS)[None, :] == (3 * head + piece)[:, None])).astype(BF16)
    const = lambda *_: (0, 0)
    return pl.pallas_call(
        functools.partial(_inproj_kernel, tm=tm, qscale=qscale),
        grid=(B, S // tm),
        in_specs=[
            pl.BlockSpec((1, tm, D), lambda b, s: (b, s, 0)),
            pl.BlockSpec((1, D), const),
            pl.BlockSpec((D, n_nat), const),
            pl.BlockSpec((2 * FOX_WIDTH, D), const),
            pl.BlockSpec((1, GATE_COLS), const),
            pl.BlockSpec((tm, tm), const),
            pl.BlockSpec((3 * GATE_COLS, GATE_COLS), const),
        ],
        out_specs=[
            pl.BlockSpec((1, tm, FOX_WIDTH), lambda b, s: (b, s, 0)),
            pl.BlockSpec((1, tm, 4 * MLSTM_WIDTH), lambda b, s: (b, s, 0)),
            pl.BlockSpec((1, 2 * FOX_WIDTH, tm), lambda b, s: (b, 0, s)),
            pl.BlockSpec((1, tm, GATE_COLS), lambda b, s: (b, s, 0)),
            pl.BlockSpec((1, GATE_COLS, tm), lambda b, s: (b, 0, s)),
            pl.BlockSpec((1, tm, GATE_COLS), lambda b, s: (b, s, 0)),
        ],
        out_shape=[
            jax.ShapeDtypeStruct((B, S, FOX_WIDTH), BF16),
            jax.ShapeDtypeStruct((B, S, 4 * MLSTM_WIDTH), BF16),
            jax.ShapeDtypeStruct((B, 2 * FOX_WIDTH, S), BF16),
            jax.ShapeDtypeStruct((B, S, GATE_COLS), F32),
            jax.ShapeDtypeStruct((B, GATE_COLS, S), F32),
            jax.ShapeDtypeStruct((B, S, GATE_COLS), BF16),
        ],
        scratch_shapes=[pltpu.VMEM((1, GATE_COLS), F32)],
        compiler_params=pltpu.CompilerParams(
            dimension_semantics=("arbitrary", "arbitrary"), vmem_limit_bytes=VMEM_LIMIT),
        name="inproj",
    )(x, g_mix, w_nat, w_t, bias, tri, place)


def _fox_kernel(qt_ref, vt_ref, k_ref, caug_ref, gain_ref, o_ref,
                qm_ref, sbuf_ref, pbuf_ref, m_ref, acc_ref, *, tb, nq, group):
    j = pl.program_id(1)
    hd2 = 2 * FOX_HEAD_DIM

    row = lax.broadcasted_iota(jnp.int32, (hd2, tb), 0)
    sel = []
    for r in range(2):
        head = 2 * j + r
        for qi in range(nq):
            qt = qt_ref[0, :, qi * tb:(qi + 1) * tb]
            qm_ref[r, :, qi * tb:(qi + 1) * tb] = jnp.where(
                row >= FOX_HEAD_DIM * r, jnp.where(row < FOX_HEAD_DIM * (r + 1), qt, 0), 0).astype(BF16)
        sel.append(jnp.where(row >= 3 * head, jnp.where(row < 3 * head + 3, -1.0, 0.0), 0.0).astype(BF16))
    ones_rows = jnp.ones((ACC_ROWS - FOX_HEAD_DIM, tb), BF16)
    neg = jnp.full((1, tb), NEG_BIG, F32)

    def start_of(idx):
        return idx * tb if isinstance(idx, int) else pl.multiple_of(idx * tb, tb)

    def stage1(pair, slot, diagonal):
        qi, u = pair
        ks, qs = start_of(u), start_of(qi)
        lhs = jnp.concatenate([k_ref[0, pl.ds(ks, tb), :], caug_ref[0, pl.ds(ks, tb), :]], axis=1)
        mcs = []
        for r in range(2):
            rhs = jnp.concatenate([qm_ref[r, :, pl.ds(qs, tb)], sel[r]], axis=0)
            st = jnp.dot(lhs, rhs, preferred_element_type=F32)
            if diagonal:
                kpos = lax.broadcasted_iota(jnp.int32, (tb, tb), 0)
                qpos = lax.broadcasted_iota(jnp.int32, (tb, tb), 1)
                st = jnp.where(kpos <= qpos, st, NEG_BIG)
            sbuf_ref[slot, r] = st
            mcs.append(jnp.max(st, axis=0, keepdims=True))
        return tuple(mcs)

    def stage2(phase, pair, slot, mcs, ms):
        qi, u = pair
        new_m, alphas = [], []
        for r in range(2):
            if ms is None:
                mn = mcs[r]
                alphas.append(None)
            else:
                m_prev = jnp.where(u == 0, neg, ms[r])
                mn = jnp.maximum(m_prev, mcs[r])
                alphas.append(jnp.exp2(m_prev - mn))
            m_ref[phase, qi, r] = mn
            pbuf_ref[slot, r] = jnp.exp2(sbuf_ref[slot, r] - mn).astype(BF16)
            new_m.append(mn)
        return tuple(new_m), tuple(alphas)

    def stage3(phase, pair, slot, alphas, accs):
        qi, u = pair
        ks = start_of(u)
        out = []
        for r in range(2):
            vt = vt_ref[0, FOX_HEAD_DIM * r:FOX_HEAD_DIM * (r + 1), pl.ds(ks, tb)]
            vt_aug = jnp.concatenate([vt, ones_rows], axis=0)
            acc = jnp.dot(vt_aug, pbuf_ref[slot, r], preferred_element_type=F32)
            if alphas[r] is not None:
                acc = alphas[r] * accs[r] + acc
            acc_ref[phase, qi, r] = acc
            out.append(acc)
        return tuple(out)

    diag = [(d, d) for d in range(nq)]
    mcs = {}
    alphas = {}
    for n in range(nq + 2):
        if n >= 2:
            stage3(0, diag[n - 2], n % 2, alphas.pop(n - 2), None)
        if 1 <= n <= nq:
            _, alphas[n - 1] = stage2(0, diag[n - 1], (n - 1) % 2, mcs.pop(n - 1), None)
        if n < nq:
            mcs[n] = stage1(diag[n], n % 2, True)

    pairs = [(qi, u) for qi in range(1, nq) for u in range(qi)]
    n_pairs = len(pairs)

    def advance(pair):
        qi, u = pair
        wrap = u + 1 == qi
        return jnp.where(wrap, qi + 1, qi), jnp.where(wrap, 0, u + 1)

    def tick(slot, p1, p2, p3, mcs, ms, alphas, accs):
        accs = stage3(1, p3, slot, alphas, accs)
        ms, alphas = stage2(1, p2, 1 - slot, mcs, ms)
        mcs = stage1(p1, slot, False)
        return mcs, ms, alphas, accs

    ms = (neg, neg)
    accs = tuple(jnp.zeros((ACC_ROWS, tb), F32) for _ in range(2))
    mcs = stage1(pairs[0], 0, False)
    ms, alphas = stage2(1, pairs[0], 0, mcs, ms)
    mcs = stage1(pairs[1], 1, False)
    n_loop = (n_pairs - 2) // group * group
    first_loop = n_pairs - n_loop
    for n in range(2, first_loop):
        mcs, ms, alphas, accs = tick(n % 2, pairs[n], pairs[n - 1], pairs[n - 2], mcs, ms, alphas, accs)
    assert group % 2 == 0

    def body(_, carry):
        p1, p2, p3, mcs, ms, alphas, accs = carry
        for g in range(group):
            mcs, ms, alphas, accs = tick((first_loop + g) % 2, p1, p2, p3, mcs, ms, alphas, accs)
            p1, p2, p3 = advance(p1), p1, p2
        return p1, p2, p3, mcs, ms, alphas, accs

    as_i32 = lambda p: tuple(jnp.asarray(x, jnp.int32) for x in p)
    carry = (as_i32(pairs[first_loop]) if first_loop < n_pairs else as_i32((nq, 0)),
             as_i32(pairs[first_loop - 1]), as_i32(pairs[first_loop - 2]), mcs, ms, alphas, accs)
    carry = lax.fori_loop(0, n_loop // group, body, carry)
    mcs, ms, alphas, accs = carry[3:]
    accs = stage3(1, pairs[-2], n_pairs % 2, alphas, accs)
    ms, alphas = stage2(1, pairs[-1], (n_pairs - 1) % 2, mcs, ms)
    stage3(1, pairs[-1], (n_pairs - 1) % 2, alphas, accs)

    for qi in range(nq):
        outs = []
        for r in range(2):
            acc = acc_ref[0, qi, r]
            if qi > 0:
                m1, m2 = m_ref[0, qi, r], m_ref[1, qi, r]
                mm = jnp.maximum(m1, m2)
                acc = jnp.exp2(m1 - mm) * acc + jnp.exp2(m2 - mm) * acc_ref[1, qi, r]
            o = acc[:FOX_HEAD_DIM] / acc[FOX_HEAD_DIM:FOX_HEAD_DIM + 1]
            msq = jnp.mean(o * o, axis=0, keepdims=True)
            outs.append(o * lax.rsqrt(msq + EPS))
        y = jnp.concatenate(outs, axis=0) * gain_ref[...]
        o_ref[0, qi * tb:(qi + 1) * tb, :] = y.T.astype(BF16)


def _fox(qvt, fk, caug, gain_b, *, tb, group):
    B, S, _ = fk.shape
    pair = 2 * FOX_HEAD_DIM
    n_pairs = FOX_HEADS // 2
    nq = S // tb
    assert nq >= 3
    return pl.pallas_call(
        functools.partial(_fox_kernel, tb=tb, nq=nq, group=group),
        grid=(B, n_pairs),
        in_specs=[
            pl.BlockSpec((1, pair, S), lambda b, j: (b, j, 0)),
            pl.BlockSpec((1, pair, S), lambda b, j: (b, n_pairs + j, 0)),
            pl.BlockSpec((1, S, pair), lambda b, j: (b, 0, j)),
            pl.BlockSpec((1, S, GATE_COLS), lambda b, j: (b, 0, 0)),
            pl.BlockSpec((pair, tb), lambda b, j: (j, 0)),
        ],
        out_specs=pl.BlockSpec((1, S, pair), lambda b, j: (b, 0, j)),
        out_shape=jax.ShapeDtypeStruct((B, S, FOX_WIDTH), BF16),
        scratch_shapes=[
            pltpu.VMEM((2, pair, S), BF16),
            pltpu.VMEM((2, 2, tb, tb), F32),
            pltpu.VMEM((2, 2, tb, tb), BF16),
            pltpu.VMEM((2, nq, 2, 1, tb), F32),
            pltpu.VMEM((2, nq, 2, ACC_ROWS, tb), F32),
        ],
        compiler_params=pltpu.CompilerParams(
            dimension_semantics=("arbitrary", "arbitrary"), vmem_limit_bytes=VMEM_LIMIT),
        name="fox",
    )(qvt, qvt, fk, caug, gain_b)


def _mlstm_kernel(mz_ref, gm_ref, gmt_ref, wconv_ref, gain_ref, o_ref,
                  ubuf_ref, state_ref, mstate_ref, *, tm):
    d = MLSTM_HEAD_DIM
    L = CHUNK
    W = MLSTM_WIDTH
    pad = 8

    @pl.when(pl.program_id(1) == 0)
    def _():
        ubuf_ref[0:pad, :] = jnp.zeros((pad, 2 * W), F32)
        state_ref[...] = jnp.zeros_like(state_ref)
        mstate_ref[...] = jnp.zeros_like(mstate_ref)

    ubuf_ref[pad:pad + tm, :] = mz_ref[0, :, :2 * W].astype(F32)
    conv = ubuf_ref[pad:pad + tm, :] * wconv_ref[CONV_WIDTH - 1:CONV_WIDTH, :]
    for jj in range(CONV_WIDTH - 1):
        shift = CONV_WIDTH - 1 - jj
        conv = conv + ubuf_ref[pad - shift:pad - shift + tm, :] * wconv_ref[jj:jj + 1, :]
    ubuf_ref[0:pad, :] = ubuf_ref[tm:tm + pad, :]
    qk = conv + conv * jnp.tanh(conv)
    q_all = qk[:, :W]
    k_all = qk[:, W:] * (d ** -0.5)

    trow = lax.broadcasted_iota(jnp.int32, (L, L), 0)
    scol = lax.broadcasted_iota(jnp.int32, (L, L), 1)
    causal = scol <= trow
    ones_blk = jnp.ones((L, LANES), BF16)
    nt = (((1,), (1,)), ((), ()))

    c_state = [state_ref[hd] for hd in range(MLSTM_HEADS)]
    m_state = [mstate_ref[hd] for hd in range(MLSTM_HEADS)]
    for c in range(tm // L):
        rs = slice(c * L, (c + 1) * L)
        for hd in range(MLSTM_HEADS):
            cs_ = slice(hd * d, (hd + 1) * d)
            qf = q_all[rs, cs_]
            kf = k_all[rs, cs_]
            v = mz_ref[0, rs, 2 * W + hd * d:2 * W + (hd + 1) * d]
            v_aug = jnp.concatenate([v, ones_blk], axis=1)
            i_col = gm_ref[0, rs, COL_ML_I + hd:COL_ML_I + hd + 1]
            b_col = gm_ref[0, rs, COL_ML_F + hd:COL_ML_F + hd + 1]
            i_row = gmt_ref[0, COL_ML_I + hd:COL_ML_I + hd + 1, rs]
            b_row = gmt_ref[0, COL_ML_F + hd:COL_ML_F + hd + 1, rs]
            emat = jnp.where(causal, i_row - b_row, NEG_BIG)
            cmb = jnp.broadcast_to(jnp.max(emat, axis=-1, keepdims=True), (L, LANES))
            ecb = jnp.broadcast_to(i_col - b_col, (L, LANES))
            bb = jnp.broadcast_to(b_col, (L, LANES))
            s = lax.dot_general(qf.astype(BF16), kf.astype(BF16), nt, preferred_element_type=F32)

            m = m_state[hd]
            c_aug = c_state[hd]
            mx = jnp.maximum(m, cmb)
            mx_last = mx[L - 1:L, :]
            w_inter = jnp.exp2(m - mx)
            w_intra = jnp.exp2(emat - mx)
            lhs = jnp.concatenate([(w_inter * qf).astype(BF16), (s * w_intra).astype(BF16)], axis=1)
            rhs = jnp.concatenate([c_aug.astype(BF16), v_aug], axis=0)
            nd = jnp.dot(lhs, rhs, preferred_element_type=F32)
            hcur = nd[:, :d] / jnp.maximum(jnp.abs(nd[:, d:]), jnp.exp2(-(bb + mx)))

            wk = jnp.exp2(ecb - mx_last) * kf
            upd = jnp.dot(wk.T.astype(BF16), v_aug, preferred_element_type=F32)
            decay = jnp.exp2(m - mx_last)
            c_state[hd] = jnp.concatenate([decay, decay], axis=1) * c_aug + upd
            m_state[hd] = bb[L - 1:L, :] + mx_last

            ms = jnp.mean(hcur * hcur, axis=-1, keepdims=True)
            og = mz_ref[0, rs, 3 * W + hd * d:3 * W + (hd + 1) * d].astype(F32)
            y = hcur * lax.rsqrt(ms + EPS) * gain_ref[:, cs_]
            o_ref[0, rs, cs_] = (y * jnp.tanh(og) + y).astype(BF16)
    for hd in range(MLSTM_HEADS):
        state_ref[hd] = c_state[hd]
        mstate_ref[hd] = m_state[hd]


def _mlstm(mz, gm, gmt, w_conv, gain, *, tm):
    B, S, _ = mz.shape
    W = MLSTM_WIDTH
    const = lambda *_: (0, 0)
    return pl.pallas_call(
        functools.partial(_mlstm_kernel, tm=tm),
        grid=(B, S // tm),
        in_specs=[
            pl.BlockSpec((1, tm, 4 * W), lambda b, s: (b, s, 0)),
            pl.BlockSpec((1, tm, GATE_COLS), lambda b, s: (b, s, 0)),
            pl.BlockSpec((1, GATE_COLS, tm), lambda b, s: (b, 0, s)),
            pl.BlockSpec((CONV_WIDTH, 2 * W), const),
            pl.BlockSpec((1, W), const),
        ],
        out_specs=pl.BlockSpec((1, tm, W), lambda b, s: (b, s, 0)),
        out_shape=jax.ShapeDtypeStruct((B, S, W), BF16),
        scratch_shapes=[
            pltpu.VMEM((tm + 8, 2 * W), F32),
            pltpu.VMEM((MLSTM_HEADS, MLSTM_HEAD_DIM, 2 * MLSTM_HEAD_DIM), F32),
            pltpu.VMEM((MLSTM_HEADS, 1, LANES), F32),
        ],
        compiler_params=pltpu.CompilerParams(
            dimension_semantics=("arbitrary", "arbitrary"), vmem_limit_bytes=VMEM_LIMIT),
        name="mlstm",
    )(mz, gm, gmt, w_conv, gain)


def _rms(x, g):
    ms = jnp.mean(x * x, axis=-1, keepdims=True)
    return x * lax.rsqrt(ms + EPS) * g


def _tail_kernel(x_ref, fox_ref, ml_ref, p_ref, wo_ref, wup_ref, wdn_ref, wpg_ref, wple_ref,
                 gmlp_ref, gple_ref, gfin_ref, o_ref, *, ff_chunk):
    mix = jnp.concatenate([fox_ref[...], ml_ref[...]], axis=1)
    x1 = x_ref[...] + jnp.dot(mix, wo_ref[...], preferred_element_type=F32)
    hm = _rms(x1, gmlp_ref[...]).astype(BF16)
    x2 = x1
    d_ff = wup_ref.shape[1]
    for f in range(d_ff // ff_chunk):
        fs = slice(f * ff_chunk, (f + 1) * ff_chunk)
        u = jnp.maximum(jnp.dot(hm, wup_ref[:, fs], preferred_element_type=F32), 0.0)
        x2 = x2 + jnp.dot((u * u).astype(BF16), wdn_ref[fs, :], preferred_element_type=F32)
    hg = _rms(x2, gple_ref[...]).astype(BF16)
    gate = _sigmoid(jnp.dot(hg, wpg_ref[...], preferred_element_type=F32))
    pe = jnp.dot(p_ref[...].astype(BF16), wple_ref[...], preferred_element_type=F32)
    x3 = x2 + gate * pe
    o_ref[...] = _rms(x3, gfin_ref[...])


def _tail(x2d, fox2d, ml2d, p2d, wo, wup, wdn, wpg, wple, gmlp, gple, gfin, *, tm):
    N, D = x2d.shape
    const = lambda i: (0, 0)
    once = pl.Buffered(1)
    wspec = lambda a: pl.BlockSpec(a.shape, const, pipeline_mode=once)
    return pl.pallas_call(
        functools.partial(_tail_kernel, ff_chunk=1024),
        grid=(N // tm,),
        in_specs=[
            pl.BlockSpec((tm, D), lambda i: (i, 0)),
            pl.BlockSpec((tm, fox2d.shape[1]), lambda i: (i, 0)),
            pl.BlockSpec((tm, ml2d.shape[1]), lambda i: (i, 0)),
            pl.BlockSpec((tm, p2d.shape[1]), lambda i: (i, 0)),
            wspec(wo), wspec(wup), wspec(wdn), wspec(wpg), wspec(wple),
            wspec(gmlp), wspec(gple), wspec(gfin),
        ],
        out_specs=pl.BlockSpec((tm, D), lambda i: (i, 0)),
        out_shape=jax.ShapeDtypeStruct((N, D), F32),
        compiler_params=pltpu.CompilerParams(
            dimension_semantics=("arbitrary",), vmem_limit_bytes=VMEM_LIMIT),
        name="tail",
    )(x2d, fox2d, ml2d, p2d, wo, wup, wdn, wpg, wple, gmlp, gple, gfin)


def _layer(x, p_i, w_in, b_fox_f, b_mlstm_i, b_mlstm_f, w_conv, g_mix, g_fox_out, g_mlstm_out,
           w_out, g_mlp, w_up, w_down, w_ple, g_ple, w_ple_gate, g_final):
    B, S, D = x.shape
    fw, mw = FOX_WIDTH, MLSTM_WIDTH
    o = 0
    cols = {}
    for name, size in (("fq", fw), ("fk", fw), ("fv", fw), ("ff", FOX_HEADS), ("mq", mw), ("mk", mw),
                       ("mv", mw), ("mi", MLSTM_HEADS), ("mf", MLSTM_HEADS), ("mo", mw)):
        cols[name] = w_in[:, o:o + size]
        o += size
    gate_w = jnp.concatenate([cols["ff"], cols["mi"], cols["mf"]], axis=1)
    gate_w = jnp.pad(gate_w, ((0, 0), (0, GATE_COLS - gate_w.shape[1])))
    w_nat = jnp.concatenate([cols["fk"], cols["mq"], cols["mk"], cols["mv"], 0.5 * cols["mo"], gate_w],
                            axis=1).astype(BF16)
    w_t = jnp.concatenate([cols["fq"], cols["fv"]], axis=1).T.astype(BF16)
    bias = jnp.concatenate([b_fox_f, b_mlstm_i, b_mlstm_f])
    bias = jnp.pad(bias, (0, GATE_COLS - bias.shape[0]))[None, :].astype(F32)

    fk, mz, qvt, gm, gmt, caug = _inproj(x, g_mix[None, :], w_nat, w_t, bias, tm=512)

    tb = 512
    gain_b = jnp.broadcast_to(g_fox_out.astype(F32)[:, None], (fw, tb))
    fox = _fox(qvt, fk, caug, gain_b, tb=tb, group=8)
    ml = _mlstm(mz, gm, gmt, 0.5 * w_conv.astype(F32), 0.5 * g_mlstm_out[None, :].astype(F32), tm=512)

    out = _tail(x.reshape(B * S, D), fox.reshape(B * S, fw), ml.reshape(B * S, mw),
                p_i.reshape(B * S, p_i.shape[-1]),
                w_out.astype(BF16), w_up.astype(BF16), w_down.astype(BF16),
                w_ple_gate.astype(BF16), w_ple.astype(BF16),
                g_mlp[None, :], g_ple[None, :], g_final[None, :], tm=512)
    return out.reshape(B, S, D)


def kernel(x, p, w_in, b_fox_f, b_mlstm_i, b_mlstm_f, w_conv, g_mix, g_fox_out, g_mlstm_out, w_out,
           g_mlp, w_up, w_down, w_ple, g_ple, w_ple_gate, g_final):
    assert w_in.shape[0] == 1, "single-layer kernel: the final RMSNorm is fused into the layer"
    return _layer(x, p[0], w_in[0], b_fox_f[0], b_mlstm_i[0], b_mlstm_f[0], w_conv[0], g_mix[0],
                  g_fox_out[0], g_mlstm_out[0], w_out[0], g_mlp[0], w_up[0], w_down[0], w_ple[0],
                  g_ple[0], w_ple_gate[0], g_final)
```

```python
import functools
import math

import jax
import jax.numpy as jnp
from jax import lax
from jax.experimental import pallas as pl
from jax.experimental.pallas import tpu as pltpu

F32 = jnp.float32
BF16 = jnp.bfloat16

EPS = 1e-6
FOX_HEADS = 8
FOX_HEAD_DIM = 64
FOX_WIDTH = FOX_HEADS * FOX_HEAD_DIM
ACC_ROWS = FOX_HEAD_DIM + 16
MLSTM_HEADS = 4
MLSTM_HEAD_DIM = 128
MLSTM_WIDTH = MLSTM_HEADS * MLSTM_HEAD_DIM
CONV_WIDTH = 4
CHUNK = 128
LANES = 128
GATE_COLS = LANES
LOG2E = 1.4426950408889634
NEG_BIG = -1e30
VMEM_LIMIT = 56 * 1024 * 1024

COL_FOX_F = 0
COL_ML_I = FOX_HEADS
COL_ML_F = FOX_HEADS + MLSTM_HEADS


def _split3(x):
    hi = x.astype(BF16)
    r1 = x - hi.astype(F32)
    mid = r1.astype(BF16)
    lo = (r1 - mid.astype(F32)).astype(BF16)
    return hi, mid, lo


def _log_sigmoid(x):
    return jnp.minimum(x, 0.0) - jnp.log1p(jnp.exp(-jnp.abs(x)))


def _sigmoid(x):
    return 0.5 * jnp.tanh(0.5 * x) + 0.5


def _inproj_kernel(x_ref, g_ref, wn_ref, wt_ref, bias_ref, tri_ref, place_ref,
                   fk_ref, mz_ref, qvt_ref, gm_ref, gmt_ref, caug_ref, carry_ref, *, tm, qscale):
    @pl.when(pl.program_id(1) == 0)
    def _():
        carry_ref[...] = jnp.zeros_like(carry_ref)

    x = x_ref[0]
    ms = jnp.mean(x * x, axis=-1, keepdims=True)
    h = (x * lax.rsqrt(ms + EPS) * g_ref[...]).astype(BF16)

    z = jnp.dot(h, wn_ref[...], preferred_element_type=F32)
    fk_ref[0] = z[:, :FOX_WIDTH].astype(BF16)
    mz_ref[0] = z[:, FOX_WIDTH:FOX_WIDTH + 4 * MLSTM_WIDTH].astype(BF16)

    nt = (((1,), (1,)), ((), ()))
    qt = lax.dot_general(wt_ref[:FOX_WIDTH, :], h, nt, preferred_element_type=F32)
    qvt_ref[0, :FOX_WIDTH, :] = (qt * qscale).astype(BF16)
    vt = lax.dot_general(wt_ref[FOX_WIDTH:, :], h, nt, preferred_element_type=F32)
    qvt_ref[0, FOX_WIDTH:, :] = vt.astype(BF16)

    g = z[:, FOX_WIDTH + 4 * MLSTM_WIDTH:] + bias_ref[...]
    logf = _log_sigmoid(g)
    hi, mid, lo = _split3(logf)
    cs3 = jnp.dot(tri_ref[...], jnp.concatenate([hi, mid, lo], axis=1), preferred_element_type=F32)
    cs = cs3[:, :GATE_COLS] + cs3[:, GATE_COLS:2 * GATE_COLS] + cs3[:, 2 * GATE_COLS:]
    pieces = []
    for c in range(tm // CHUNK):
        blk = cs[c * CHUNK:(c + 1) * CHUNK]
        if c > 0:
            blk = blk - cs[c * CHUNK - 1:c * CHUNK]
        pieces.append(blk)
    bloc = jnp.concatenate(pieces, axis=0)
    cfox = (carry_ref[...] + cs) * LOG2E
    carry_ref[...] = carry_ref[...] + cs[tm - 1:tm]

    col = lax.broadcasted_iota(jnp.int32, (tm, GATE_COLS), 1)
    gm = jnp.where(col < COL_ML_I, cfox, jnp.where(col < COL_ML_F, g, bloc) * LOG2E)
    gm_ref[0] = gm
    gmt_ref[0] = gm.T
    chi, cmid, clo = _split3(cfox)
    caug = jnp.dot(jnp.concatenate([chi, cmid, clo], axis=1), place_ref[...], preferred_element_type=F32)
    caug_ref[0] = caug.astype(BF16)


def _inproj(x, g_mix, w_nat, w_t, bias, *, tm):
    B, S, D = x.shape
    n_nat = w_nat.shape[1]
    qscale = FOX_HEAD_DIM ** -0.5 * LOG2E
    tri = (jnp.arange(tm)[:, None] >= jnp.arange(tm)[None, :]).astype(BF16)
    rows = jnp.arange(3 * GATE_COLS)
    piece, head = rows // GATE_COLS, rows % GATE_COLS
    place = ((head[:, None] < FOX_HEADS) & (jnp.arange(GATE_COLS)[None, :] == (3 * head + piece)[:, None])).astype(BF16)
    const = lambda *_: (0, 0)
    return pl.pallas_call(
        functools.partial(_inproj_kernel, tm=tm, qscale=qscale),
        grid=(B, S // tm),
        in_specs=[
            pl.BlockSpec((1, tm, D), lambda b, s: (b, s, 0)),
            pl.BlockSpec((1, D), const),
            pl.BlockSpec((D, n_nat), const),
            pl.BlockSpec((2 * FOX_WIDTH, D), const),
            pl.BlockSpec((1, GATE_COLS), const),
            pl.BlockSpec((tm, tm), const),
            pl.BlockSpec((3 * GATE_COLS, GATE_COLS), const),
        ],
        out_specs=[
            pl.BlockSpec((1, tm, FOX_WIDTH), lambda b, s: (b, s, 0)),
            pl.BlockSpec((1, tm, 4 * MLSTM_WIDTH), lambda b, s: (b, s, 0)),
            pl.BlockSpec((1, 2 * FOX_WIDTH, tm), lambda b, s: (b, 0, s)),
            pl.BlockSpec((1, tm, GATE_COLS), lambda b, s: (b, s, 0)),
            pl.BlockSpec((1, GATE_COLS, tm), lambda b, s: (b, 0, s)),
            pl.BlockSpec((1, tm, GATE_COLS), lambda b, s: (b, s, 0)),
        ],
        out_shape=[
            jax.ShapeDtypeStruct((B, S, FOX_WIDTH), BF16),
            jax.ShapeDtypeStruct((B, S, 4 * MLSTM_WIDTH), BF16),
            jax.ShapeDtypeStruct((B, 2 * FOX_WIDTH, S), BF16),
            jax.ShapeDtypeStruct((B, S, GATE_COLS), F32),
            jax.ShapeDtypeStruct((B, GATE_COLS, S), F32),
            jax.ShapeDtypeStruct((B, S, GATE_COLS), BF16),
        ],
        scratch_shapes=[pltpu.VMEM((1, GATE_COLS), F32)],
        compiler_params=pltpu.CompilerParams(
            dimension_semantics=("arbitrary", "arbitrary"), vmem_limit_bytes=VMEM_LIMIT),
        name="inproj",
    )(x, g_mix, w_nat, w_t, bias, tri, place)


def _fox_kernel(qt_ref, vt_ref, k_ref, caug_ref, gain_ref, o_ref,
                qm_ref, sbuf_ref, pbuf_ref, m_ref, acc_ref, *, tb, nq, group):
    j = pl.program_id(1)
    hd2 = 2 * FOX_HEAD_DIM

    row = lax.broadcasted_iota(jnp.int32, (hd2, tb), 0)
    sel = []
    for r in range(2):
        head = 2 * j + r
        for qi in range(nq):
            qt = qt_ref[0, :, qi * tb:(qi + 1) * tb]
            qm_ref[r, :, qi * tb:(qi + 1) * tb] = jnp.where(
                row >= FOX_HEAD_DIM * r, jnp.where(row < FOX_HEAD_DIM * (r + 1), qt, 0), 0).astype(BF16)
        sel.append(jnp.where(row >= 3 * head, jnp.where(row < 3 * head + 3, -1.0, 0.0), 0.0).astype(BF16))
    ones_rows = jnp.ones((ACC_ROWS - FOX_HEAD_DIM, tb), BF16)
    neg = jnp.full((1, tb), NEG_BIG, F32)

    def start_of(idx):
        return idx * tb if isinstance(idx, int) else pl.multiple_of(idx * tb, tb)

    def stage1(pair, slot, diagonal):
        qi, u = pair
        ks, qs = start_of(u), start_of(qi)
        lhs = jnp.concatenate([k_ref[0, pl.ds(ks, tb), :], caug_ref[0, pl.ds(ks, tb), :]], axis=1)
        mcs = []
        for r in range(2):
            rhs = jnp.concatenate([qm_ref[r, :, pl.ds(qs, tb)], sel[r]], axis=0)
            st = jnp.dot(lhs, rhs, preferred_element_type=F32)
            if diagonal:
                kpos = lax.broadcasted_iota(jnp.int32, (tb, tb), 0)
                qpos = lax.broadcasted_iota(jnp.int32, (tb, tb), 1)
                st = jnp.where(kpos <= qpos, st, NEG_BIG)
            sbuf_ref[slot, r] = st
            mcs.append(jnp.max(st, axis=0, keepdims=True))
        return tuple(mcs)

    def stage2(phase, pair, slot, mcs, ms):
        qi, u = pair
        new_m, alphas = [], []
        for r in range(2):
            if ms is None:
                mn = mcs[r]
                alphas.append(None)
            else:
                m_prev = jnp.where(u == 0, neg, ms[r])
                mn = jnp.maximum(m_prev, mcs[r])
                alphas.append(jnp.exp2(m_prev - mn))
            m_ref[phase, qi, r] = mn
            pbuf_ref[slot, r] = jnp.exp2(sbuf_ref[slot, r] - mn).astype(BF16)
            new_m.append(mn)
        return tuple(new_m), tuple(alphas)

    def stage3(phase, pair, slot, alphas, accs):
        qi, u = pair
        ks = start_of(u)
        out = []
        for r in range(2):
            vt = vt_ref[0, FOX_HEAD_DIM * r:FOX_HEAD_DIM * (r + 1), pl.ds(ks, tb)]
            vt_aug = jnp.concatenate([vt, ones_rows], axis=0)
            acc = jnp.dot(vt_aug, pbuf_ref[slot, r], preferred_element_type=F32)
            if alphas[r] is not None:
                acc = alphas[r] * accs[r] + acc
            acc_ref[phase, qi, r] = acc
            out.append(acc)
        return tuple(out)

    diag = [(d, d) for d in range(nq)]
    mcs = {}
    alphas = {}
    for n in range(nq + 2):
        if n >= 2:
            stage3(0, diag[n - 2], n % 2, alphas.pop(n - 2), None)
        if 1 <= n <= nq:
            _, alphas[n - 1] = stage2(0, diag[n - 1], (n - 1) % 2, mcs.pop(n - 1), None)
        if n < nq:
            mcs[n] = stage1(diag[n], n % 2, True)

    pairs = [(qi, u) for qi in range(1, nq) for u in range(qi)]
    n_pairs = len(pairs)

    def advance(pair):
        qi, u = pair
        wrap = u + 1 == qi
        return jnp.where(wrap, qi + 1, qi), jnp.where(wrap, 0, u + 1)

    def tick(slot, p1, p2, p3, mcs, ms, alphas, accs):
        accs = stage3(1, p3, slot, alphas, accs)
        ms, alphas = stage2(1, p2, 1 - slot, mcs, ms)
        mcs = stage1(p1, slot, False)
        return mcs, ms, alphas, accs

    ms = (neg, neg)
    accs = tuple(jnp.zeros((ACC_ROWS, tb), F32) for _ in range(2))
    mcs = stage1(pairs[0], 0, False)
    ms, alphas = stage2(1, pairs[0], 0, mcs, ms)
    mcs = stage1(pairs[1], 1, False)
    n_loop = (n_pairs - 2) // group * group
    first_loop = n_pairs - n_loop
    for n in range(2, first_loop):
        mcs, ms, alphas, accs = tick(n % 2, pairs[n], pairs[n - 1], pairs[n - 2], mcs, ms, alphas, accs)
    assert group % 2 == 0

    def body(_, carry):
        p1, p2, p3, mcs, ms, alphas, accs = carry
        for g in range(group):
            mcs, ms, alphas, accs = tick((first_loop + g) % 2, p1, p2, p3, mcs, ms, alphas, accs)
            p1, p2, p3 = advance(p1), p1, p2
        return p1, p2, p3, mcs, ms, alphas, accs

    as_i32 = lambda p: tuple(jnp.asarray(x, jnp.int32) for x in p)
    carry = (as_i32(pairs[first_loop]) if first_loop < n_pairs else as_i32((nq, 0)),
             as_i32(pairs[first_loop - 1]), as_i32(pairs[first_loop - 2]), mcs, ms, alphas, accs)
    carry = lax.fori_loop(0, n_loop // group, body, carry)
    mcs, ms, alphas, accs = carry[3:]
    accs = stage3(1, pairs[-2], n_pairs % 2, alphas, accs)
    ms, alphas = stage2(1, pairs[-1], (n_pairs - 1) % 2, mcs, ms)
    stage3(1, pairs[-1], (n_pairs - 1) % 2, alphas, accs)

    for qi in range(nq):
        outs = []
        for r in range(2):
            acc = acc_ref[0, qi, r]
            if qi > 0:
                m1, m2 = m_ref[0, qi, r], m_ref[1, qi, r]
                mm = jnp.maximum(m1, m2)
                acc = jnp.exp2(m1 - mm) * acc + jnp.exp2(m2 - mm) * acc_ref[1, qi, r]
            o = acc[:FOX_HEAD_DIM] / acc[FOX_HEAD_DIM:FOX_HEAD_DIM + 1]
            msq = jnp.mean(o * o, axis=0, keepdims=True)
            outs.append(o * lax.rsqrt(msq + EPS))
        y = jnp.concatenate(outs, axis=0) * gain_ref[...]
        o_ref[0, qi * tb:(qi + 1) * tb, :] = y.T.astype(BF16)


def _fox(qvt, fk, caug, gain_b, *, tb, group):
    B, S, _ = fk.shape
    pair = 2 * FOX_HEAD_DIM
    n_pairs = FOX_HEADS // 2
    nq = S // tb
    assert nq >= 3
    return pl.pallas_call(
        functools.partial(_fox_kernel, tb=tb, nq=nq, group=group),
        grid=(B, n_pairs),
        in_specs=[
            pl.BlockSpec((1, pair, S), lambda b, j: (b, j, 0)),
            pl.BlockSpec((1, pair, S), lambda b, j: (b, n_pairs + j, 0)),
            pl.BlockSpec((1, S, pair), lambda b, j: (b, 0, j)),
            pl.BlockSpec((1, S, GATE_COLS), lambda b, j: (b, 0, 0)),
            pl.BlockSpec((pair, tb), lambda b, j: (j, 0)),
        ],
        out_specs=pl.BlockSpec((1, S, pair), lambda b, j: (b, 0, j)),
        out_shape=jax.ShapeDtypeStruct((B, S, FOX_WIDTH), BF16),
        scratch_shapes=[
            pltpu.VMEM((2, pair, S), BF16),
            pltpu.VMEM((2, 2, tb, tb), F32),
            pltpu.VMEM((2, 2, tb, tb), BF16),
            pltpu.VMEM((2, nq, 2, 1, tb), F32),
            pltpu.VMEM((2, nq, 2, ACC_ROWS, tb), F32),
        ],
        compiler_params=pltpu.CompilerParams(
            dimension_semantics=("arbitrary", "arbitrary"), vmem_limit_bytes=VMEM_LIMIT),
        name="fox",
    )(qvt, qvt, fk, caug, gain_b)


def _mlstm_kernel(mz_ref, gm_ref, gmt_ref, wconv_ref, gain_ref, o_ref,
                  ubuf_ref, state_ref, mstate_ref, *, tm):
    d = MLSTM_HEAD_DIM
    L = CHUNK
    W = MLSTM_WIDTH
    pad = 8

    @pl.when(pl.program_id(1) == 0)
    def _():
        ubuf_ref[0:pad, :] = jnp.zeros((pad, 2 * W), F32)
        state_ref[...] = jnp.zeros_like(state_ref)
        mstate_ref[...] = jnp.zeros_like(mstate_ref)

    ubuf_ref[pad:pad + tm, :] = mz_ref[0, :, :2 * W].astype(F32)
    conv = ubuf_ref[pad:pad + tm, :] * wconv_ref[CONV_WIDTH - 1:CONV_WIDTH, :]
    for jj in range(CONV_WIDTH - 1):
        shift = CONV_WIDTH - 1 - jj
        conv = conv + ubuf_ref[pad - shift:pad - shift + tm, :] * wconv_ref[jj:jj + 1, :]
    ubuf_ref[0:pad, :] = ubuf_ref[tm:tm + pad, :]
    qk = conv + conv * jnp.tanh(conv)
    q_all = qk[:, :W]
    k_all = qk[:, W:] * (d ** -0.5)

    trow = lax.broadcasted_iota(jnp.int32, (L, L), 0)
    scol = lax.broadcasted_iota(jnp.int32, (L, L), 1)
    causal = scol <= trow
    ones_blk = jnp.ones((L, LANES), BF16)
    nt = (((1,), (1,)), ((), ()))

    c_state = [state_ref[hd] for hd in range(MLSTM_HEADS)]
    m_state = [mstate_ref[hd] for hd in range(MLSTM_HEADS)]
    for c in range(tm // L):
        rs = slice(c * L, (c + 1) * L)
        for hd in range(MLSTM_HEADS):
            cs_ = slice(hd * d, (hd + 1) * d)
            qf = q_all[rs, cs_]
            kf = k_all[rs, cs_]
            v = mz_ref[0, rs, 2 * W + hd * d:2 * W + (hd + 1) * d]
            v_aug = jnp.concatenate([v, ones_blk], axis=1)
            i_col = gm_ref[0, rs, COL_ML_I + hd:COL_ML_I + hd + 1]
            b_col = gm_ref[0, rs, COL_ML_F + hd:COL_ML_F + hd + 1]
            i_row = gmt_ref[0, COL_ML_I + hd:COL_ML_I + hd + 1, rs]
            b_row = gmt_ref[0, COL_ML_F + hd:COL_ML_F + hd + 1, rs]
            emat = jnp.where(causal, i_row - b_row, NEG_BIG)
            cmb = jnp.broadcast_to(jnp.max(emat, axis=-1, keepdims=True), (L, LANES))
            ecb = jnp.broadcast_to(i_col - b_col, (L, LANES))
            bb = jnp.broadcast_to(b_col, (L, LANES))
            s = lax.dot_general(qf.astype(BF16), kf.astype(BF16), nt, preferred_element_type=F32)

            m = m_state[hd]
            c_aug = c_state[hd]
            mx = jnp.maximum(m, cmb)
            mx_last = mx[L - 1:L, :]
            w_inter = jnp.exp2(m - mx)
            w_intra = jnp.exp2(emat - mx)
            lhs = jnp.concatenate([(w_inter * qf).astype(BF16), (s * w_intra).astype(BF16)], axis=1)
            rhs = jnp.concatenate([c_aug.astype(BF16), v_aug], axis=0)
            nd = jnp.dot(lhs, rhs, preferred_element_type=F32)
            hcur = nd[:, :d] / jnp.maximum(jnp.abs(nd[:, d:]), jnp.exp2(-(bb + mx)))

            wk = jnp.exp2(ecb - mx_last) * kf
            upd = jnp.dot(wk.T.astype(BF16), v_aug, preferred_element_type=F32)
            decay = jnp.exp2(m - mx_last)
            c_state[hd] = jnp.concatenate([decay, decay], axis=1) * c_aug + upd
            m_state[hd] = bb[L - 1:L, :] + mx_last

            ms = jnp.mean(hcur * hcur, axis=-1, keepdims=True)
            og = mz_ref[0, rs, 3 * W + hd * d:3 * W + (hd + 1) * d].astype(F32)
            y = hcur * lax.rsqrt(ms + EPS) * gain_ref[:, cs_]
            o_ref[0, rs, cs_] = (y * jnp.tanh(og) + y).astype(BF16)
    for hd in range(MLSTM_HEADS):
        state_ref[hd] = c_state[hd]
        mstate_ref[hd] = m_state[hd]


def _mlstm(mz, gm, gmt, w_conv, gain, *, tm):
    B, S, _ = mz.shape
    W = MLSTM_WIDTH
    const = lambda *_: (0, 0)
    return pl.pallas_call(
        functools.partial(_mlstm_kernel, tm=tm),
        grid=(B, S // tm),
        in_specs=[
            pl.BlockSpec((1, tm, 4 * W), lambda b, s: (b, s, 0)),
            pl.BlockSpec((1, tm, GATE_COLS), lambda b, s: (b, s, 0)),
            pl.BlockSpec((1, GATE_COLS, tm), lambda b, s: (b, 0, s)),
            pl.BlockSpec((CONV_WIDTH, 2 * W), const),
            pl.BlockSpec((1, W), const),
        ],
        out_specs=pl.BlockSpec((1, tm, W), lambda b, s: (b, s, 0)),
        out_shape=jax.ShapeDtypeStruct((B, S, W), BF16),
        scratch_shapes=[
            pltpu.VMEM((tm + 8, 2 * W), F32),
            pltpu.VMEM((MLSTM_HEADS, MLSTM_HEAD_DIM, 2 * MLSTM_HEAD_DIM), F32),
            pltpu.VMEM((MLSTM_HEADS, 1, LANES), F32),
        ],
        compiler_params=pltpu.CompilerParams(
            dimension_semantics=("arbitrary", "arbitrary"), vmem_limit_bytes=VMEM_LIMIT),
        name="mlstm",
    )(mz, gm, gmt, w_conv, gain)


def _rms(x, g):
    ms = jnp.mean(x * x, axis=-1, keepdims=True)
    return x * lax.rsqrt(ms + EPS) * g


def _tail_kernel(x_ref, fox_ref, ml_ref, p_ref, wo_ref, wup_ref, wdn_ref, wpg_ref, wple_ref,
                 gmlp_ref, gple_ref, gfin_ref, o_ref, *, ff_chunk):
    mix = jnp.concatenate([fox_ref[...], ml_ref[...]], axis=1)
    x1 = x_ref[...] + jnp.dot(mix, wo_ref[...], preferred_element_type=F32)
    hm = _rms(x1, gmlp_ref[...]).astype(BF16)
    x2 = x1
    d_ff = wup_ref.shape[1]
    for f in range(d_ff // ff_chunk):
        fs = slice(f * ff_chunk, (f + 1) * ff_chunk)
        u = jnp.maximum(jnp.dot(hm, wup_ref[:, fs], preferred_element_type=F32), 0.0)
        x2 = x2 + jnp.dot((u * u).astype(BF16), wdn_ref[fs, :], preferred_element_type=F32)
    hg = _rms(x2, gple_ref[...]).astype(BF16)
    gate = _sigmoid(jnp.dot(hg, wpg_ref[...], preferred_element_type=F32))
    pe = jnp.dot(p_ref[...].astype(BF16), wple_ref[...], preferred_element_type=F32)
    x3 = x2 + gate * pe
    o_ref[...] = _rms(x3, gfin_ref[...])


def _tail(x2d, fox2d, ml2d, p2d, wo, wup, wdn, wpg, wple, gmlp, gple, gfin, *, tm):
    N, D = x2d.shape
    const = lambda i: (0, 0)
    once = pl.Buffered(1)
    wspec = lambda a: pl.BlockSpec(a.shape, const, pipeline_mode=once)
    return pl.pallas_call(
        functools.partial(_tail_kernel, ff_chunk=1024),
        grid=(N // tm,),
        in_specs=[
            pl.BlockSpec((tm, D), lambda i: (i, 0)),
            pl.BlockSpec((tm, fox2d.shape[1]), lambda i: (i, 0)),
            pl.BlockSpec((tm, ml2d.shape[1]), lambda i: (i, 0)),
            pl.BlockSpec((tm, p2d.shape[1]), lambda i: (i, 0)),
            wspec(wo), wspec(wup), wspec(wdn), wspec(wpg), wspec(wple),
            wspec(gmlp), wspec(gple), wspec(gfin),
        ],
        out_specs=pl.BlockSpec((tm, D), lambda i: (i, 0)),
        out_shape=jax.ShapeDtypeStruct((N, D), F32),
        compiler_params=pltpu.CompilerParams(
            dimension_semantics=("arbitrary",), vmem_limit_bytes=VMEM_LIMIT),
        name="tail",
    )(x2d, fox2d, ml2d, p2d, wo, wup, wdn, wpg, wple, gmlp, gple, gfin)


def _layer(x, p_i, w_in, b_fox_f, b_mlstm_i, b_mlstm_f, w_conv, g_mix, g_fox_out, g_mlstm_out,
           w_out, g_mlp, w_up, w_down, w_ple, g_ple, w_ple_gate, g_final):
    B, S, D = x.shape
    fw, mw = FOX_WIDTH, MLSTM_WIDTH
    o = 0
    cols = {}
    for name, size in (("fq", fw), ("fk", fw), ("fv", fw), ("ff", FOX_HEADS), ("mq", mw), ("mk", mw),
                       ("mv", mw), ("mi", MLSTM_HEADS), ("mf", MLSTM_HEADS), ("mo", mw)):
        cols[name] = w_in[:, o:o + size]
        o += size
    gate_w = jnp.concatenate([cols["ff"], cols["mi"], cols["mf"]], axis=1)
    gate_w = jnp.pad(gate_w, ((0, 0), (0, GATE_COLS - gate_w.shape[1])))
    w_nat = jnp.concatenate([cols["fk"], cols["mq"], cols["mk"], cols["mv"], 0.5 * cols["mo"], gate_w],
                            axis=1).astype(BF16)
    w_t = jnp.concatenate([cols["fq"], cols["fv"]], axis=1).T.astype(BF16)
    bias = jnp.concatenate([b_fox_f, b_mlstm_i, b_mlstm_f])
    bias = jnp.pad(bias, (0, GATE_COLS - bias.shape[0]))[None, :].astype(F32)

    fk, mz, qvt, gm, gmt, caug = _inproj(x, g_mix[None, :], w_nat, w_t, bias, tm=512)

    tb = 512
    gain_b = jnp.broadcast_to(g_fox_out.astype(F32)[:, None], (fw, tb))
    fox = _fox(qvt, fk, caug, gain_b, tb=tb, group=24)
    ml = _mlstm(mz, gm, gmt, 0.5 * w_conv.astype(F32), 0.5 * g_mlstm_out[None, :].astype(F32), tm=512)

    out = _tail(x.reshape(B * S, D), fox.reshape(B * S, fw), ml.reshape(B * S, mw),
                p_i.reshape(B * S, p_i.shape[-1]),
                w_out.astype(BF16), w_up.astype(BF16), w_down.astype(BF16),
                w_ple_gate.astype(BF16), w_ple.astype(BF16),
                g_mlp[None, :], g_ple[None, :], g_final[None, :], tm=512)
    return out.reshape(B, S, D)


def kernel(x, p, w_in, b_fox_f, b_mlstm_i, b_mlstm_f, w_conv, g_mix, g_fox_out, g_mlstm_out, w_out,
           g_mlp, w_up, w_down, w_ple, g_ple, w_ple_gate, g_final):
    assert w_in.shape[0] == 1, "single-layer kernel: the final RMSNorm is fused into the layer"
    return _layer(x, p[0], w_in[0], b_fox_f[0], b_mlstm_i[0], b_mlstm_f[0], w_conv[0], g_mix[0],
                  g_fox_out[0], g_mlstm_out[0], w_out[0], g_mlp[0], w_up[0], w_down[0], w_ple[0],
                  g_ple[0], w_ple_gate[0], g_final)
```

```python
import functools
import math

import jax
import jax.numpy as jnp
from jax import lax
from jax.experimental import pallas as pl
from jax.experimental.pallas import tpu as pltpu

F32 = jnp.float32
BF16 = jnp.bfloat16

EPS = 1e-6
FOX_HEADS = 8
FOX_HEAD_DIM = 64
FOX_WIDTH = FOX_HEADS * FOX_HEAD_DIM
ACC_ROWS = FOX_HEAD_DIM + 16
MLSTM_HEADS = 4
MLSTM_HEAD_DIM = 128
MLSTM_WIDTH = MLSTM_HEADS * MLSTM_HEAD_DIM
CONV_WIDTH = 4
CHUNK = 128
LANES = 128
GATE_COLS = LANES
LOG2E = 1.4426950408889634
NEG_BIG = -1e30
VMEM_LIMIT = 56 * 1024 * 1024

COL_FOX_F = 0
COL_ML_I = FOX_HEADS
COL_ML_F = FOX_HEADS + MLSTM_HEADS


def _split3(x):
    hi = x.astype(BF16)
    r1 = x - hi.astype(F32)
    mid = r1.astype(BF16)
    lo = (r1 - mid.astype(F32)).astype(BF16)
    return hi, mid, lo


def _log_sigmoid(x):
    return jnp.minimum(x, 0.0) - jnp.log1p(jnp.exp(-jnp.abs(x)))


def _sigmoid(x):
    return 0.5 * jnp.tanh(0.5 * x) + 0.5


def _inproj_kernel(x_ref, g_ref, wn_ref, wt_ref, bias_ref, tri_ref, place_ref,
                   fk_ref, mz_ref, qvt_ref, gm_ref, gmt_ref, caug_ref, carry_ref, *, tm, qscale):
    @pl.when(pl.program_id(1) == 0)
    def _():
        carry_ref[...] = jnp.zeros_like(carry_ref)

    x = x_ref[0]
    ms = jnp.mean(x * x, axis=-1, keepdims=True)
    h = (x * lax.rsqrt(ms + EPS) * g_ref[...]).astype(BF16)

    z = jnp.dot(h, wn_ref[...], preferred_element_type=F32)
    fk_ref[0] = z[:, :FOX_WIDTH].astype(BF16)
    mz_ref[0] = z[:, FOX_WIDTH:FOX_WIDTH + 4 * MLSTM_WIDTH].astype(BF16)

    nt = (((1,), (1,)), ((), ()))
    qt = lax.dot_general(wt_ref[:FOX_WIDTH, :], h, nt, preferred_element_type=F32)
    qvt_ref[0, :FOX_WIDTH, :] = (qt * qscale).astype(BF16)
    vt = lax.dot_general(wt_ref[FOX_WIDTH:, :], h, nt, preferred_element_type=F32)
    qvt_ref[0, FOX_WIDTH:, :] = vt.astype(BF16)

    g = z[:, FOX_WIDTH + 4 * MLSTM_WIDTH:] + bias_ref[...]
    logf = _log_sigmoid(g)
    hi, mid, lo = _split3(logf)
    cs3 = jnp.dot(tri_ref[...], jnp.concatenate([hi, mid, lo], axis=1), preferred_element_type=F32)
    cs = cs3[:, :GATE_COLS] + cs3[:, GATE_COLS:2 * GATE_COLS] + cs3[:, 2 * GATE_COLS:]
    pieces = []
    for c in range(tm // CHUNK):
        blk = cs[c * CHUNK:(c + 1) * CHUNK]
        if c > 0:
            blk = blk - cs[c * CHUNK - 1:c * CHUNK]
        pieces.append(blk)
    bloc = jnp.concatenate(pieces, axis=0)
    cfox = (carry_ref[...] + cs) * LOG2E
    carry_ref[...] = carry_ref[...] + cs[tm - 1:tm]

    col = lax.broadcasted_iota(jnp.int32, (tm, GATE_COLS), 1)
    gm = jnp.where(col < COL_ML_I, cfox, jnp.where(col < COL_ML_F, g, bloc) * LOG2E)
    gm_ref[0] = gm
    gmt_ref[0] = gm.T
    chi, cmid, clo = _split3(cfox)
    caug = jnp.dot(jnp.concatenate([chi, cmid, clo], axis=1), place_ref[...], preferred_element_type=F32)
    caug_ref[0] = caug.astype(BF16)


def _inproj(x, g_mix, w_nat, w_t, bias, *, tm):
    B, S, D = x.shape
    n_nat = w_nat.shape[1]
    qscale = FOX_HEAD_DIM ** -0.5 * LOG2E
    tri = (jnp.arange(tm)[:, None] >= jnp.arange(tm)[None, :]).astype(BF16)
    rows = jnp.arange(3 * GATE_COLS)
    piece, head = rows // GATE_COLS, rows % GATE_COLS
    place = ((head[:, None] < FOX_HEADS) & (jnp.arange(GATE_COLS)[None, :] == (3 * head + piece)[:, None])).astype(BF16)
    const = lambda *_: (0, 0)
    return pl.pallas_call(
        functools.partial(_inproj_kernel, tm=tm, qscale=qscale),
        grid=(B, S // tm),
        in_specs=[
            pl.BlockSpec((1, tm, D), lambda b, s: (b, s, 0)),
            pl.BlockSpec((1, D), const),
            pl.BlockSpec((D, n_nat), const),
            pl.BlockSpec((2 * FOX_WIDTH, D), const),
            pl.BlockSpec((1, GATE_COLS), const),
            pl.BlockSpec((tm, tm), const),
            pl.BlockSpec((3 * GATE_COLS, GATE_COLS), const),
        ],
        out_specs=[
            pl.BlockSpec((1, tm, FOX_WIDTH), lambda b, s: (b, s, 0)),
            pl.BlockSpec((1, tm, 4 * MLSTM_WIDTH), lambda b, s: (b, s, 0)),
            pl.BlockSpec((1, 2 * FOX_WIDTH, tm), lambda b, s: (b, 0, s)),
            pl.BlockSpec((1, tm, GATE_COLS), lambda b, s: (b, s, 0)),
            pl.BlockSpec((1, GATE_COLS, tm), lambda b, s: (b, 0, s)),
            pl.BlockSpec((1, tm, GATE_COLS), lambda b, s: (b, s, 0)),
        ],
        out_shape=[
            jax.ShapeDtypeStruct((B, S, FOX_WIDTH), BF16),
            jax.ShapeDtypeStruct((B, S, 4 * MLSTM_WIDTH), BF16),
            jax.ShapeDtypeStruct((B, 2 * FOX_WIDTH, S), BF16),
            jax.ShapeDtypeStruct((B, S, GATE_COLS), F32),
            jax.ShapeDtypeStruct((B, GATE_COLS, S), F32),
            jax.ShapeDtypeStruct((B, S, GATE_COLS), BF16),
        ],
        scratch_shapes=[pltpu.VMEM((1, GATE_COLS), F32)],
        compiler_params=pltpu.CompilerParams(
            dimension_semantics=("arbitrary", "arbitrary"), vmem_limit_bytes=VMEM_LIMIT),
        name="inproj",
    )(x, g_mix, w_nat, w_t, bias, tri, place)


def _fox_kernel(qt_ref, vt_ref, k_ref, caug_ref, gain_ref, o_ref, qm_ref, sbuf_ref, pbuf_ref, *, tb, nq):
    j = pl.program_id(1)
    hd2 = 2 * FOX_HEAD_DIM
    th = tb // 2

    row = lax.broadcasted_iota(jnp.int32, (hd2, tb), 0)
    sel = []
    for r in range(2):
        head = 2 * j + r
        for qi in range(nq):
            qt = qt_ref[0, :, qi * tb:(qi + 1) * tb]
            qm_ref[r, :, qi * tb:(qi + 1) * tb] = jnp.where(
                row >= FOX_HEAD_DIM * r, jnp.where(row < FOX_HEAD_DIM * (r + 1), qt, 0), 0).astype(BF16)
        sel.append(jnp.where(row >= 3 * head, jnp.where(row < 3 * head + 3, -1.0, 0.0), 0.0).astype(BF16))
    ones_rows = jnp.ones((ACC_ROWS - FOX_HEAD_DIM, tb), BF16)

    def stage1(pair, slot):
        qi, u = pair
        ks, qs = u * tb, qi * tb
        lhs = jnp.concatenate([k_ref[0, ks:ks + tb, :], caug_ref[0, ks:ks + tb, :]], axis=1)
        mcs = []
        for r in range(2):
            rhs = jnp.concatenate([qm_ref[r, :, qs:qs + tb], sel[r]], axis=0)
            if u == qi:
                st_a = jnp.dot(lhs[:th], rhs[:, :th], preferred_element_type=F32)
                st_b = jnp.dot(lhs, rhs[:, th:], preferred_element_type=F32)
                ka = lax.broadcasted_iota(jnp.int32, (th, th), 0)
                qa = lax.broadcasted_iota(jnp.int32, (th, th), 1)
                st_a = jnp.where(ka <= qa, st_a, NEG_BIG)
                kb = lax.broadcasted_iota(jnp.int32, (tb, th), 0)
                qb = lax.broadcasted_iota(jnp.int32, (tb, th), 1) + th
                st_b = jnp.where(kb <= qb, st_b, NEG_BIG)
                sbuf_ref[slot, r, :th, :th] = st_a
                sbuf_ref[slot, r, :, th:] = st_b
                mcs.append(jnp.concatenate([jnp.max(st_a, axis=0, keepdims=True),
                                            jnp.max(st_b, axis=0, keepdims=True)], axis=1))
            else:
                st = jnp.dot(lhs, rhs, preferred_element_type=F32)
                sbuf_ref[slot, r] = st
                mcs.append(jnp.max(st, axis=0, keepdims=True))
        return tuple(mcs)

    def stage2(pair, slot, mcs, ms):
        qi, u = pair
        new_m, alphas = [], []
        for r in range(2):
            if u == 0:
                mn = mcs[r]
                alphas.append(None)
            else:
                mn = jnp.maximum(ms[r], mcs[r])
                alphas.append(jnp.exp2(ms[r] - mn))
            if u == qi:
                pbuf_ref[slot, r, :th, :th] = jnp.exp2(sbuf_ref[slot, r, :th, :th] - mn[:, :th]).astype(BF16)
                pbuf_ref[slot, r, :, th:] = jnp.exp2(sbuf_ref[slot, r, :, th:] - mn[:, th:]).astype(BF16)
            else:
                pbuf_ref[slot, r] = jnp.exp2(sbuf_ref[slot, r] - mn).astype(BF16)
            new_m.append(mn)
        return tuple(new_m), tuple(alphas)

    def stage3(pair, slot, alphas, accs):
        qi, u = pair
        ks = u * tb
        out = []
        for r in range(2):
            vt = vt_ref[0, FOX_HEAD_DIM * r:FOX_HEAD_DIM * (r + 1), ks:ks + tb]
            vt_aug = jnp.concatenate([vt, ones_rows], axis=0)
            if u == qi:
                acc = jnp.concatenate(
                    [jnp.dot(vt_aug[:, :th], pbuf_ref[slot, r, :th, :th], preferred_element_type=F32),
                     jnp.dot(vt_aug, pbuf_ref[slot, r, :, th:], preferred_element_type=F32)], axis=1)
            else:
                acc = jnp.dot(vt_aug, pbuf_ref[slot, r], preferred_element_type=F32)
            if alphas[r] is not None:
                acc = alphas[r] * accs[r] + acc
            out.append(acc)
        return tuple(out)

    def finalize(qi, accs):
        outs = []
        for r in range(2):
            o = accs[r][:FOX_HEAD_DIM] / accs[r][FOX_HEAD_DIM:FOX_HEAD_DIM + 1]
            msq = jnp.mean(o * o, axis=0, keepdims=True)
            outs.append(o * lax.rsqrt(msq + EPS))
        y = jnp.concatenate(outs, axis=0) * gain_ref[...]
        o_ref[0, qi * tb:(qi + 1) * tb, :] = y.T.astype(BF16)

    pairs = [(qi, u) for qi in range(nq) for u in range(qi + 1)]
    n_pairs = len(pairs)
    mcs, alphas, ms, accs = {}, {}, None, None
    for n in range(n_pairs + 2):
        if n >= 2:
            accs = stage3(pairs[n - 2], n % 2, alphas.pop(n - 2), accs)
            if pairs[n - 2][1] == pairs[n - 2][0]:
                finalize(pairs[n - 2][0], accs)
        if 1 <= n <= n_pairs:
            ms, alphas[n - 1] = stage2(pairs[n - 1], (n - 1) % 2, mcs.pop(n - 1), ms)
        if n < n_pairs:
            mcs[n] = stage1(pairs[n], n % 2)


def _fox(qvt, fk, caug, gain_b, *, tb):
    B, S, _ = fk.shape
    pair = 2 * FOX_HEAD_DIM
    n_pairs = FOX_HEADS // 2
    nq = S // tb
    return pl.pallas_call(
        functools.partial(_fox_kernel, tb=tb, nq=nq),
        grid=(B, n_pairs),
        in_specs=[
            pl.BlockSpec((1, pair, S), lambda b, j: (b, j, 0)),
            pl.BlockSpec((1, pair, S), lambda b, j: (b, n_pairs + j, 0)),
            pl.BlockSpec((1, S, pair), lambda b, j: (b, 0, j)),
            pl.BlockSpec((1, S, GATE_COLS), lambda b, j: (b, 0, 0)),
            pl.BlockSpec((pair, tb), lambda b, j: (j, 0)),
        ],
        out_specs=pl.BlockSpec((1, S, pair), lambda b, j: (b, 0, j)),
        out_shape=jax.ShapeDtypeStruct((B, S, FOX_WIDTH), BF16),
        scratch_shapes=[
            pltpu.VMEM((2, pair, S), BF16),
            pltpu.VMEM((2, 2, tb, tb), F32),
            pltpu.VMEM((2, 2, tb, tb), BF16),
        ],
        compiler_params=pltpu.CompilerParams(
            dimension_semantics=("arbitrary", "arbitrary"), vmem_limit_bytes=VMEM_LIMIT),
        name="fox",
    )(qvt, qvt, fk, caug, gain_b)


def _mlstm_kernel(mz_ref, gm_ref, gmt_ref, wconv_ref, gain_ref, o_ref,
                  ubuf_ref, state_ref, mstate_ref, *, tm):
    d = MLSTM_HEAD_DIM
    L = CHUNK
    W = MLSTM_WIDTH
    pad = 8

    @pl.when(pl.program_id(1) == 0)
    def _():
        ubuf_ref[0:pad, :] = jnp.zeros((pad, 2 * W), F32)
        state_ref[...] = jnp.zeros_like(state_ref)
        mstate_ref[...] = jnp.zeros_like(mstate_ref)

    ubuf_ref[pad:pad + tm, :] = mz_ref[0, :, :2 * W].astype(F32)
    conv = ubuf_ref[pad:pad + tm, :] * wconv_ref[CONV_WIDTH - 1:CONV_WIDTH, :]
    for jj in range(CONV_WIDTH - 1):
        shift = CONV_WIDTH - 1 - jj
        conv = conv + ubuf_ref[pad - shift:pad - shift + tm, :] * wconv_ref[jj:jj + 1, :]
    ubuf_ref[0:pad, :] = ubuf_ref[tm:tm + pad, :]
    qk = conv + conv * jnp.tanh(conv)
    q_all = qk[:, :W]
    k_all = qk[:, W:] * (d ** -0.5)

    trow = lax.broadcasted_iota(jnp.int32, (L, L), 0)
    scol = lax.broadcasted_iota(jnp.int32, (L, L), 1)
    causal = scol <= trow
    ones_blk = jnp.ones((L, LANES), BF16)
    nt = (((1,), (1,)), ((), ()))

    c_state = [state_ref[hd] for hd in range(MLSTM_HEADS)]
    m_state = [mstate_ref[hd] for hd in range(MLSTM_HEADS)]
    for c in range(tm // L):
        rs = slice(c * L, (c + 1) * L)
        for hd in range(MLSTM_HEADS):
            cs_ = slice(hd * d, (hd + 1) * d)
            qf = q_all[rs, cs_]
            kf = k_all[rs, cs_]
            v = mz_ref[0, rs, 2 * W + hd * d:2 * W + (hd + 1) * d]
            v_aug = jnp.concatenate([v, ones_blk], axis=1)
            i_col = gm_ref[0, rs, COL_ML_I + hd:COL_ML_I + hd + 1]
            b_col = gm_ref[0, rs, COL_ML_F + hd:COL_ML_F + hd + 1]
            i_row = gmt_ref[0, COL_ML_I + hd:COL_ML_I + hd + 1, rs]
            b_row = gmt_ref[0, COL_ML_F + hd:COL_ML_F + hd + 1, rs]
            emat = jnp.where(causal, i_row - b_row, NEG_BIG)
            cmb = jnp.broadcast_to(jnp.max(emat, axis=-1, keepdims=True), (L, LANES))
            ecb = jnp.broadcast_to(i_col - b_col, (L, LANES))
            bb = jnp.broadcast_to(b_col, (L, LANES))
            s = lax.dot_general(qf.astype(BF16), kf.astype(BF16), nt, preferred_element_type=F32)

            m = m_state[hd]
            c_aug = c_state[hd]
            mx = jnp.maximum(m, cmb)
            mx_last = mx[L - 1:L, :]
            w_inter = jnp.exp2(m - mx)
            w_intra = jnp.exp2(emat - mx)
            lhs = jnp.concatenate([(w_inter * qf).astype(BF16), (s * w_intra).astype(BF16)], axis=1)
            rhs = jnp.concatenate([c_aug.astype(BF16), v_aug], axis=0)
            nd = jnp.dot(lhs, rhs, preferred_element_type=F32)
            hcur = nd[:, :d] / jnp.maximum(jnp.abs(nd[:, d:]), jnp.exp2(-(bb + mx)))

            wk = jnp.exp2(ecb - mx_last) * kf
            upd = jnp.dot(wk.T.astype(BF16), v_aug, preferred_element_type=F32)
            decay = jnp.exp2(m - mx_last)
            c_state[hd] = jnp.concatenate([decay, decay], axis=1) * c_aug + upd
            m_state[hd] = bb[L - 1:L, :] + mx_last

            ms = jnp.mean(hcur * hcur, axis=-1, keepdims=True)
            og = mz_ref[0, rs, 3 * W + hd * d:3 * W + (hd + 1) * d].astype(F32)
            y = hcur * lax.rsqrt(ms + EPS) * gain_ref[:, cs_]
            o_ref[0, rs, cs_] = (y * jnp.tanh(og) + y).astype(BF16)
    for hd in range(MLSTM_HEADS):
        state_ref[hd] = c_state[hd]
        mstate_ref[hd] = m_state[hd]


def _mlstm(mz, gm, gmt, w_conv, gain, *, tm):
    B, S, _ = mz.shape
    W = MLSTM_WIDTH
    const = lambda *_: (0, 0)
    return pl.pallas_call(
        functools.partial(_mlstm_kernel, tm=tm),
        grid=(B, S // tm),
        in_specs=[
            pl.BlockSpec((1, tm, 4 * W), lambda b, s: (b, s, 0)),
            pl.BlockSpec((1, tm, GATE_COLS), lambda b, s: (b, s, 0)),
            pl.BlockSpec((1, GATE_COLS, tm), lambda b, s: (b, 0, s)),
            pl.BlockSpec((CONV_WIDTH, 2 * W), const),
            pl.BlockSpec((1, W), const),
        ],
        out_specs=pl.BlockSpec((1, tm, W), lambda b, s: (b, s, 0)),
        out_shape=jax.ShapeDtypeStruct((B, S, W), BF16),
        scratch_shapes=[
            pltpu.VMEM((tm + 8, 2 * W), F32),
            pltpu.VMEM((MLSTM_HEADS, MLSTM_HEAD_DIM, 2 * MLSTM_HEAD_DIM), F32),
            pltpu.VMEM((MLSTM_HEADS, 1, LANES), F32),
        ],
        compiler_params=pltpu.CompilerParams(
            dimension_semantics=("arbitrary", "arbitrary"), vmem_limit_bytes=VMEM_LIMIT),
        name="mlstm",
    )(mz, gm, gmt, w_conv, gain)


def _rms(x, g):
    ms = jnp.mean(x * x, axis=-1, keepdims=True)
    return x * lax.rsqrt(ms + EPS) * g


def _tail_kernel(x_ref, fox_ref, ml_ref, p_ref, wo_ref, wup_ref, wdn_ref, wpg_ref, wple_ref,
                 gmlp_ref, gple_ref, gfin_ref, o_ref, *, ff_chunk):
    mix = jnp.concatenate([fox_ref[...], ml_ref[...]], axis=1)
    x1 = x_ref[...] + jnp.dot(mix, wo_ref[...], preferred_element_type=F32)
    hm = _rms(x1, gmlp_ref[...]).astype(BF16)
    x2 = x1
    d_ff = wup_ref.shape[1]
    for f in range(d_ff // ff_chunk):
        fs = slice(f * ff_chunk, (f + 1) * ff_chunk)
        u = jnp.maximum(jnp.dot(hm, wup_ref[:, fs], preferred_element_type=F32), 0.0)
        x2 = x2 + jnp.dot((u * u).astype(BF16), wdn_ref[fs, :], preferred_element_type=F32)
    hg = _rms(x2, gple_ref[...]).astype(BF16)
    gate = _sigmoid(jnp.dot(hg, wpg_ref[...], preferred_element_type=F32))
    pe = jnp.dot(p_ref[...].astype(BF16), wple_ref[...], preferred_element_type=F32)
    x3 = x2 + gate * pe
    o_ref[...] = _rms(x3, gfin_ref[...])


def _tail(x2d, fox2d, ml2d, p2d, wo, wup, wdn, wpg, wple, gmlp, gple, gfin, *, tm):
    N, D = x2d.shape
    const = lambda i: (0, 0)
    once = pl.Buffered(1)
    wspec = lambda a: pl.BlockSpec(a.shape, const, pipeline_mode=once)
    return pl.pallas_call(
        functools.partial(_tail_kernel, ff_chunk=1024),
        grid=(N // tm,),
        in_specs=[
            pl.BlockSpec((tm, D), lambda i: (i, 0)),
            pl.BlockSpec((tm, fox2d.shape[1]), lambda i: (i, 0)),
            pl.BlockSpec((tm, ml2d.shape[1]), lambda i: (i, 0)),
            pl.BlockSpec((tm, p2d.shape[1]), lambda i: (i, 0)),
            wspec(wo), wspec(wup), wspec(wdn), wspec(wpg), wspec(wple),
            wspec(gmlp), wspec(gple), wspec(gfin),
        ],
        out_specs=pl.BlockSpec((tm, D), lambda i: (i, 0)),
        out_shape=jax.ShapeDtypeStruct((N, D), F32),
        compiler_params=pltpu.CompilerParams(
            dimension_semantics=("arbitrary",), vmem_limit_bytes=VMEM_LIMIT),
        name="tail",
    )(x2d, fox2d, ml2d, p2d, wo, wup, wdn, wpg, wple, gmlp, gple, gfin)


def _layer(x, p_i, w_in, b_fox_f, b_mlstm_i, b_mlstm_f, w_conv, g_mix, g_fox_out, g_mlstm_out,
           w_out, g_mlp, w_up, w_down, w_ple, g_ple, w_ple_gate, g_final):
    B, S, D = x.shape
    fw, mw = FOX_WIDTH, MLSTM_WIDTH
    o = 0
    cols = {}
    for name, size in (("fq", fw), ("fk", fw), ("fv", fw), ("ff", FOX_HEADS), ("mq", mw), ("mk", mw),
                       ("mv", mw), ("mi", MLSTM_HEADS), ("mf", MLSTM_HEADS), ("mo", mw)):
        cols[name] = w_in[:, o:o + size]
        o += size
    gate_w = jnp.concatenate([cols["ff"], cols["mi"], cols["mf"]], axis=1)
    gate_w = jnp.pad(gate_w, ((0, 0), (0, GATE_COLS - gate_w.shape[1])))
    w_nat = jnp.concatenate([cols["fk"], cols["mq"], cols["mk"], cols["mv"], 0.5 * cols["mo"], gate_w],
                            axis=1).astype(BF16)
    w_t = jnp.concatenate([cols["fq"], cols["fv"]], axis=1).T.astype(BF16)
    bias = jnp.concatenate([b_fox_f, b_mlstm_i, b_mlstm_f])
    bias = jnp.pad(bias, (0, GATE_COLS - bias.shape[0]))[None, :].astype(F32)

    fk, mz, qvt, gm, gmt, caug = _inproj(x, g_mix[None, :], w_nat, w_t, bias, tm=512)

    tb = 512
    gain_b = jnp.broadcast_to(g_fox_out.astype(F32)[:, None], (fw, tb))
    fox = _fox(qvt, fk, caug, gain_b, tb=tb)
    ml = _mlstm(mz, gm, gmt, 0.5 * w_conv.astype(F32), 0.5 * g_mlstm_out[None, :].astype(F32), tm=512)

    out = _tail(x.reshape(B * S, D), fox.reshape(B * S, fw), ml.reshape(B * S, mw),
                p_i.reshape(B * S, p_i.shape[-1]),
                w_out.astype(BF16), w_up.astype(BF16), w_down.astype(BF16),
                w_ple_gate.astype(BF16), w_ple.astype(BF16),
                g_mlp[None, :], g_ple[None, :], g_final[None, :], tm=1024)
    return out.reshape(B, S, D)


def kernel(x, p, w_in, b_fox_f, b_mlstm_i, b_mlstm_f, w_conv, g_mix, g_fox_out, g_mlstm_out, w_out,
           g_mlp, w_up, w_down, w_ple, g_ple, w_ple_gate, g_final):
    assert w_in.shape[0] == 1, "single-layer kernel: the final RMSNorm is fused into the layer"
    return _layer(x, p[0], w_in[0], b_fox_f[0], b_mlstm_i[0], b_mlstm_f[0], w_conv[0], g_mix[0],
                  g_fox_out[0], g_mlstm_out[0], w_out[0], g_mlp[0], w_up[0], w_down[0], w_ple[0],
                  g_ple[0], w_ple_gate[0], g_final)
```

```python
import functools
import math

import jax
import jax.numpy as jnp
from jax import lax
from jax.experimental import pallas as pl
from jax.experimental.pallas import tpu as pltpu

F32 = jnp.float32
BF16 = jnp.bfloat16

EPS = 1e-6
FOX_HEADS = 8
FOX_HEAD_DIM = 64
FOX_WIDTH = FOX_HEADS * FOX_HEAD_DIM
ACC_ROWS = FOX_HEAD_DIM + 16
MLSTM_HEADS = 4
MLSTM_HEAD_DIM = 128
MLSTM_WIDTH = MLSTM_HEADS * MLSTM_HEAD_DIM
CONV_WIDTH = 4
CHUNK = 128
LANES = 128
GATE_COLS = LANES
LOG2E = 1.4426950408889634
NEG_BIG = -1e30
VMEM_LIMIT = 56 * 1024 * 1024

COL_FOX_F = 0
COL_ML_I = FOX_HEADS
COL_ML_F = FOX_HEADS + MLSTM_HEADS


def _split3(x):
    hi = x.astype(BF16)
    r1 = x - hi.astype(F32)
    mid = r1.astype(BF16)
    lo = (r1 - mid.astype(F32)).astype(BF16)
    return hi, mid, lo


def _log_sigmoid(x):
    return jnp.minimum(x, 0.0) - jnp.log1p(jnp.exp(-jnp.abs(x)))


def _sigmoid(x):
    return 0.5 * jnp.tanh(0.5 * x) + 0.5


def _inproj_kernel(x_ref, g_ref, wn_ref, wt_ref, bias_ref, tri_ref, place_ref, wconv_ref,
                   fk_ref, mz_ref, qvt_ref, gm_ref, gmt_ref, caug_ref, carry_ref, ubuf_ref, *, tm, qscale):
    W = MLSTM_WIDTH
    pad = 8

    @pl.when(pl.program_id(1) == 0)
    def _():
        carry_ref[...] = jnp.zeros_like(carry_ref)
        ubuf_ref[0:pad, :] = jnp.zeros((pad, 2 * W), F32)

    x = x_ref[0]
    ms = jnp.mean(x * x, axis=-1, keepdims=True)
    h = (x * lax.rsqrt(ms + EPS) * g_ref[...]).astype(BF16)

    z = jnp.dot(h, wn_ref[...], preferred_element_type=F32)
    fk_ref[0] = z[:, :FOX_WIDTH].astype(BF16)
    mz_ref[0, :, 2 * W:] = z[:, FOX_WIDTH + 2 * W:FOX_WIDTH + 4 * W].astype(BF16)

    ubuf_ref[pad:pad + tm, :] = z[:, FOX_WIDTH:FOX_WIDTH + 2 * W]
    conv = ubuf_ref[pad:pad + tm, :] * wconv_ref[CONV_WIDTH - 1:CONV_WIDTH, :]
    for jj in range(CONV_WIDTH - 1):
        shift = CONV_WIDTH - 1 - jj
        conv = conv + ubuf_ref[pad - shift:pad - shift + tm, :] * wconv_ref[jj:jj + 1, :]
    ubuf_ref[0:pad, :] = ubuf_ref[tm:tm + pad, :]
    qk = conv + conv * jnp.tanh(conv)
    mz_ref[0, :, :W] = qk[:, :W].astype(BF16)
    mz_ref[0, :, W:2 * W] = (qk[:, W:] * (MLSTM_HEAD_DIM ** -0.5)).astype(BF16)

    nt = (((1,), (1,)), ((), ()))
    qt = lax.dot_general(wt_ref[:FOX_WIDTH, :], h, nt, preferred_element_type=F32)
    qvt_ref[0, :FOX_WIDTH, :] = (qt * qscale).astype(BF16)
    vt = lax.dot_general(wt_ref[FOX_WIDTH:, :], h, nt, preferred_element_type=F32)
    qvt_ref[0, FOX_WIDTH:, :] = vt.astype(BF16)

    g = z[:, FOX_WIDTH + 4 * MLSTM_WIDTH:] + bias_ref[...]
    logf = _log_sigmoid(g)
    hi, mid, lo = _split3(logf)
    cs3 = jnp.dot(tri_ref[...], jnp.concatenate([hi, mid, lo], axis=1), preferred_element_type=F32)
    cs = cs3[:, :GATE_COLS] + cs3[:, GATE_COLS:2 * GATE_COLS] + cs3[:, 2 * GATE_COLS:]
    pieces = []
    for c in range(tm // CHUNK):
        blk = cs[c * CHUNK:(c + 1) * CHUNK]
        if c > 0:
            blk = blk - cs[c * CHUNK - 1:c * CHUNK]
        pieces.append(blk)
    bloc = jnp.concatenate(pieces, axis=0)
    cfox = (carry_ref[...] + cs) * LOG2E
    carry_ref[...] = carry_ref[...] + cs[tm - 1:tm]

    col = lax.broadcasted_iota(jnp.int32, (tm, GATE_COLS), 1)
    gm = jnp.where(col < COL_ML_I, cfox, jnp.where(col < COL_ML_F, g, bloc) * LOG2E)
    gm_ref[0] = gm
    gmt_ref[0] = gm.T
    chi, cmid, clo = _split3(cfox)
    caug = jnp.dot(jnp.concatenate([chi, cmid, clo], axis=1), place_ref[...], preferred_element_type=F32)
    caug_ref[0] = caug.astype(BF16)


def _inproj(x, g_mix, w_nat, w_t, bias, w_conv, *, tm):
    B, S, D = x.shape
    n_nat = w_nat.shape[1]
    qscale = FOX_HEAD_DIM ** -0.5 * LOG2E
    tri = (jnp.arange(tm)[:, None] >= jnp.arange(tm)[None, :]).astype(BF16)
    rows = jnp.arange(3 * GATE_COLS)
    piece, head = rows // GATE_COLS, rows % GATE_COLS
    place = ((head[:, None] < FOX_HEADS) & (jnp.arange(GATE_COLS)[None, :] == (3 * head + piece)[:, None])).astype(BF16)
    const = lambda *_: (0, 0)
    return pl.pallas_call(
        functools.partial(_inproj_kernel, tm=tm, qscale=qscale),
        grid=(B, S // tm),
        in_specs=[
            pl.BlockSpec((1, tm, D), lambda b, s: (b, s, 0)),
            pl.BlockSpec((1, D), const),
            pl.BlockSpec((D, n_nat), const),
            pl.BlockSpec((2 * FOX_WIDTH, D), const),
            pl.BlockSpec((1, GATE_COLS), const),
            pl.BlockSpec((tm, tm), const),
            pl.BlockSpec((3 * GATE_COLS, GATE_COLS), const),
            pl.BlockSpec((CONV_WIDTH, 2 * MLSTM_WIDTH), const),
        ],
        out_specs=[
            pl.BlockSpec((1, tm, FOX_WIDTH), lambda b, s: (b, s, 0)),
            pl.BlockSpec((1, tm, 4 * MLSTM_WIDTH), lambda b, s: (b, s, 0)),
            pl.BlockSpec((1, 2 * FOX_WIDTH, tm), lambda b, s: (b, 0, s)),
            pl.BlockSpec((1, tm, GATE_COLS), lambda b, s: (b, s, 0)),
            pl.BlockSpec((1, GATE_COLS, tm), lambda b, s: (b, 0, s)),
            pl.BlockSpec((1, tm, GATE_COLS), lambda b, s: (b, s, 0)),
        ],
        out_shape=[
            jax.ShapeDtypeStruct((B, S, FOX_WIDTH), BF16),
            jax.ShapeDtypeStruct((B, S, 4 * MLSTM_WIDTH), BF16),
            jax.ShapeDtypeStruct((B, 2 * FOX_WIDTH, S), BF16),
            jax.ShapeDtypeStruct((B, S, GATE_COLS), F32),
            jax.ShapeDtypeStruct((B, GATE_COLS, S), F32),
            jax.ShapeDtypeStruct((B, S, GATE_COLS), BF16),
        ],
        scratch_shapes=[pltpu.VMEM((1, GATE_COLS), F32),
                        pltpu.VMEM((tm + 8, 2 * MLSTM_WIDTH), F32)],
        compiler_params=pltpu.CompilerParams(
            dimension_semantics=("arbitrary", "arbitrary"), vmem_limit_bytes=VMEM_LIMIT),
        name="inproj",
    )(x, g_mix, w_nat, w_t, bias, tri, place, w_conv)


def _fox_kernel(qt_ref, vt_ref, k_ref, caug_ref, gain_ref, o_ref, qm_ref, sbuf_ref, pbuf_ref, *, tb, nq):
    j = pl.program_id(1)
    hd2 = 2 * FOX_HEAD_DIM
    th = tb // 2

    row = lax.broadcasted_iota(jnp.int32, (hd2, tb), 0)
    sel = []
    for r in range(2):
        head = 2 * j + r
        for qi in range(nq):
            qt = qt_ref[0, :, qi * tb:(qi + 1) * tb]
            qm_ref[r, :, qi * tb:(qi + 1) * tb] = jnp.where(
                row >= FOX_HEAD_DIM * r, jnp.where(row < FOX_HEAD_DIM * (r + 1), qt, 0), 0).astype(BF16)
        sel.append(jnp.where(row >= 3 * head, jnp.where(row < 3 * head + 3, -1.0, 0.0), 0.0).astype(BF16))
    ones_rows = jnp.ones((ACC_ROWS - FOX_HEAD_DIM, tb), BF16)

    def stage1(pair, slot):
        qi, u = pair
        ks, qs = u * tb, qi * tb
        lhs = jnp.concatenate([k_ref[0, ks:ks + tb, :], caug_ref[0, ks:ks + tb, :]], axis=1)
        mcs = []
        for r in range(2):
            rhs = jnp.concatenate([qm_ref[r, :, qs:qs + tb], sel[r]], axis=0)
            if u == qi:
                st_a = jnp.dot(lhs[:th], rhs[:, :th], preferred_element_type=F32)
                st_b = jnp.dot(lhs, rhs[:, th:], preferred_element_type=F32)
                ka = lax.broadcasted_iota(jnp.int32, (th, th), 0)
                qa = lax.broadcasted_iota(jnp.int32, (th, th), 1)
                st_a = jnp.where(ka <= qa, st_a, NEG_BIG)
                kb = lax.broadcasted_iota(jnp.int32, (tb, th), 0)
                qb = lax.broadcasted_iota(jnp.int32, (tb, th), 1) + th
                st_b = jnp.where(kb <= qb, st_b, NEG_BIG)
                sbuf_ref[slot, r, :th, :th] = st_a
                sbuf_ref[slot, r, :, th:] = st_b
                mcs.append(jnp.concatenate([jnp.max(st_a, axis=0, keepdims=True),
                                            jnp.max(st_b, axis=0, keepdims=True)], axis=1))
            else:
                st = jnp.dot(lhs, rhs, preferred_element_type=F32)
                sbuf_ref[slot, r] = st
                mcs.append(jnp.max(st, axis=0, keepdims=True))
        return tuple(mcs)

    def stage2(pair, slot, mcs, ms):
        qi, u = pair
        new_m, alphas = [], []
        for r in range(2):
            if u == 0:
                mn = mcs[r]
                alphas.append(None)
            else:
                mn = jnp.maximum(ms[r], mcs[r])
                alphas.append(jnp.exp2(ms[r] - mn))
            if u == qi:
                pbuf_ref[slot, r, :th, :th] = jnp.exp2(sbuf_ref[slot, r, :th, :th] - mn[:, :th]).astype(BF16)
                pbuf_ref[slot, r, :, th:] = jnp.exp2(sbuf_ref[slot, r, :, th:] - mn[:, th:]).astype(BF16)
            else:
                pbuf_ref[slot, r] = jnp.exp2(sbuf_ref[slot, r] - mn).astype(BF16)
            new_m.append(mn)
        return tuple(new_m), tuple(alphas)

    def stage3(pair, slot, alphas, accs):
        qi, u = pair
        ks = u * tb
        out = []
        for r in range(2):
            vt = vt_ref[0, FOX_HEAD_DIM * r:FOX_HEAD_DIM * (r + 1), ks:ks + tb]
            vt_aug = jnp.concatenate([vt, ones_rows], axis=0)
            if u == qi:
                acc = jnp.concatenate(
                    [jnp.dot(vt_aug[:, :th], pbuf_ref[slot, r, :th, :th], preferred_element_type=F32),
                     jnp.dot(vt_aug, pbuf_ref[slot, r, :, th:], preferred_element_type=F32)], axis=1)
            else:
                acc = jnp.dot(vt_aug, pbuf_ref[slot, r], preferred_element_type=F32)
            if alphas[r] is not None:
                acc = alphas[r] * accs[r] + acc
            out.append(acc)
        return tuple(out)

    def finalize(qi, accs):
        outs = []
        for r in range(2):
            o = accs[r][:FOX_HEAD_DIM] / accs[r][FOX_HEAD_DIM:FOX_HEAD_DIM + 1]
            msq = jnp.mean(o * o, axis=0, keepdims=True)
            outs.append(o * lax.rsqrt(msq + EPS))
        y = jnp.concatenate(outs, axis=0) * gain_ref[...]
        o_ref[0, qi * tb:(qi + 1) * tb, :] = y.T.astype(BF16)

    pairs = [(qi, u) for qi in range(nq) for u in range(qi + 1)]
    n_pairs = len(pairs)
    mcs, alphas, ms, accs = {}, {}, None, None
    for n in range(n_pairs + 2):
        if n >= 2:
            accs = stage3(pairs[n - 2], n % 2, alphas.pop(n - 2), accs)
            if pairs[n - 2][1] == pairs[n - 2][0]:
                finalize(pairs[n - 2][0], accs)
        if 1 <= n <= n_pairs:
            ms, alphas[n - 1] = stage2(pairs[n - 1], (n - 1) % 2, mcs.pop(n - 1), ms)
        if n < n_pairs:
            mcs[n] = stage1(pairs[n], n % 2)


def _fox(qvt, fk, caug, gain_b, *, tb):
    B, S, _ = fk.shape
    pair = 2 * FOX_HEAD_DIM
    n_pairs = FOX_HEADS // 2
    nq = S // tb
    return pl.pallas_call(
        functools.partial(_fox_kernel, tb=tb, nq=nq),
        grid=(B, n_pairs),
        in_specs=[
            pl.BlockSpec((1, pair, S), lambda b, j: (b, j, 0)),
            pl.BlockSpec((1, pair, S), lambda b, j: (b, n_pairs + j, 0)),
            pl.BlockSpec((1, S, pair), lambda b, j: (b, 0, j)),
            pl.BlockSpec((1, S, GATE_COLS), lambda b, j: (b, 0, 0)),
            pl.BlockSpec((pair, tb), lambda b, j: (j, 0)),
        ],
        out_specs=pl.BlockSpec((1, S, pair), lambda b, j: (b, 0, j)),
        out_shape=jax.ShapeDtypeStruct((B, S, FOX_WIDTH), BF16),
        scratch_shapes=[
            pltpu.VMEM((2, pair, S), BF16),
            pltpu.VMEM((2, 2, tb, tb), F32),
            pltpu.VMEM((2, 2, tb, tb), BF16),
        ],
        compiler_params=pltpu.CompilerParams(
            dimension_semantics=("arbitrary", "arbitrary"), vmem_limit_bytes=VMEM_LIMIT),
        name="fox",
    )(qvt, qvt, fk, caug, gain_b)


def _mlstm_kernel(mz_ref, gm_ref, gmt_ref, gain_ref, o_ref, state_ref, mstate_ref, *, tm):
    d = MLSTM_HEAD_DIM
    L = CHUNK
    W = MLSTM_WIDTH

    @pl.when(pl.program_id(1) == 0)
    def _():
        state_ref[...] = jnp.zeros_like(state_ref)
        mstate_ref[...] = jnp.zeros_like(mstate_ref)

    trow = lax.broadcasted_iota(jnp.int32, (L, L), 0)
    scol = lax.broadcasted_iota(jnp.int32, (L, L), 1)
    causal = scol <= trow
    ones_blk = jnp.ones((L, LANES), BF16)
    nt = (((1,), (1,)), ((), ()))

    c_state = [state_ref[hd] for hd in range(MLSTM_HEADS)]
    m_state = [mstate_ref[hd] for hd in range(MLSTM_HEADS)]
    for c in range(tm // L):
        rs = slice(c * L, (c + 1) * L)
        for hd in range(MLSTM_HEADS):
            cs_ = slice(hd * d, (hd + 1) * d)
            qb = mz_ref[0, rs, cs_]
            kb = mz_ref[0, rs, W + hd * d:W + (hd + 1) * d]
            qf = qb.astype(F32)
            kf = kb.astype(F32)
            v = mz_ref[0, rs, 2 * W + hd * d:2 * W + (hd + 1) * d]
            v_aug = jnp.concatenate([v, ones_blk], axis=1)
            i_col = gm_ref[0, rs, COL_ML_I + hd:COL_ML_I + hd + 1]
            b_col = gm_ref[0, rs, COL_ML_F + hd:COL_ML_F + hd + 1]
            i_row = gmt_ref[0, COL_ML_I + hd:COL_ML_I + hd + 1, rs]
            b_row = gmt_ref[0, COL_ML_F + hd:COL_ML_F + hd + 1, rs]
            emat = jnp.where(causal, i_row - b_row, NEG_BIG)
            cmb = jnp.broadcast_to(jnp.max(emat, axis=-1, keepdims=True), (L, LANES))
            ecb = jnp.broadcast_to(i_col - b_col, (L, LANES))
            bb = jnp.broadcast_to(b_col, (L, LANES))
            s = lax.dot_general(qb, kb, nt, preferred_element_type=F32)

            m = m_state[hd]
            c_aug = c_state[hd]
            mx = jnp.maximum(m, cmb)
            mx_last = mx[L - 1:L, :]
            w_inter = jnp.exp2(m - mx)
            w_intra = jnp.exp2(emat - mx)
            lhs = jnp.concatenate([(w_inter * qf).astype(BF16), (s * w_intra).astype(BF16)], axis=1)
            rhs = jnp.concatenate([c_aug.astype(BF16), v_aug], axis=0)
            nd = jnp.dot(lhs, rhs, preferred_element_type=F32)
            hcur = nd[:, :d] / jnp.maximum(jnp.abs(nd[:, d:]), jnp.exp2(-(bb + mx)))

            wk = jnp.exp2(ecb - mx_last) * kf
            upd = jnp.dot(wk.T.astype(BF16), v_aug, preferred_element_type=F32)
            decay = jnp.exp2(m - mx_last)
            c_state[hd] = jnp.concatenate([decay, decay], axis=1) * c_aug + upd
            m_state[hd] = bb[L - 1:L, :] + mx_last

            ms = jnp.mean(hcur * hcur, axis=-1, keepdims=True)
            og = mz_ref[0, rs, 3 * W + hd * d:3 * W + (hd + 1) * d].astype(F32)
            y = hcur * lax.rsqrt(ms + EPS) * gain_ref[:, cs_]
            o_ref[0, rs, cs_] = (y * jnp.tanh(og) + y).astype(BF16)
    for hd in range(MLSTM_HEADS):
        state_ref[hd] = c_state[hd]
        mstate_ref[hd] = m_state[hd]


def _mlstm(mz, gm, gmt, gain, *, tm):
    B, S, _ = mz.shape
    W = MLSTM_WIDTH
    const = lambda *_: (0, 0)
    return pl.pallas_call(
        functools.partial(_mlstm_kernel, tm=tm),
        grid=(B, S // tm),
        in_specs=[
            pl.BlockSpec((1, tm, 4 * W), lambda b, s: (b, s, 0)),
            pl.BlockSpec((1, tm, GATE_COLS), lambda b, s: (b, s, 0)),
            pl.BlockSpec((1, GATE_COLS, tm), lambda b, s: (b, 0, s)),
            pl.BlockSpec((1, W), const),
        ],
        out_specs=pl.BlockSpec((1, tm, W), lambda b, s: (b, s, 0)),
        out_shape=jax.ShapeDtypeStruct((B, S, W), BF16),
        scratch_shapes=[
            pltpu.VMEM((MLSTM_HEADS, MLSTM_HEAD_DIM, 2 * MLSTM_HEAD_DIM), F32),
            pltpu.VMEM((MLSTM_HEADS, 1, LANES), F32),
        ],
        compiler_params=pltpu.CompilerParams(
            dimension_semantics=("arbitrary", "arbitrary"), vmem_limit_bytes=VMEM_LIMIT),
        name="mlstm",
    )(mz, gm, gmt, gain)


def _rms(x, g):
    ms = jnp.mean(x * x, axis=-1, keepdims=True)
    return x * lax.rsqrt(ms + EPS) * g


def _tail_kernel(x_ref, fox_ref, ml_ref, p_ref, wo_ref, wup_ref, wdn_ref, wpg_ref, wple_ref,
                 gmlp_ref, gple_ref, gfin_ref, o_ref, *, ff_chunk):
    mix = jnp.concatenate([fox_ref[...], ml_ref[...]], axis=1)
    x1 = x_ref[...] + jnp.dot(mix, wo_ref[...], preferred_element_type=F32)
    hm = _rms(x1, gmlp_ref[...]).astype(BF16)
    x2 = x1
    d_ff = wup_ref.shape[1]
    for f in range(d_ff // ff_chunk):
        fs = slice(f * ff_chunk, (f + 1) * ff_chunk)
        u = jnp.maximum(jnp.dot(hm, wup_ref[:, fs], preferred_element_type=F32), 0.0)
        x2 = x2 + jnp.dot((u * u).astype(BF16), wdn_ref[fs, :], preferred_element_type=F32)
    hg = _rms(x2, gple_ref[...]).astype(BF16)
    gate = _sigmoid(jnp.dot(hg, wpg_ref[...], preferred_element_type=F32))
    pe = jnp.dot(p_ref[...].astype(BF16), wple_ref[...], preferred_element_type=F32)
    x3 = x2 + gate * pe
    o_ref[...] = _rms(x3, gfin_ref[...])


def _tail(x2d, fox2d, ml2d, p2d, wo, wup, wdn, wpg, wple, gmlp, gple, gfin, *, tm):
    N, D = x2d.shape
    const = lambda i: (0, 0)
    once = pl.Buffered(1)
    wspec = lambda a: pl.BlockSpec(a.shape, const, pipeline_mode=once)
    return pl.pallas_call(
        functools.partial(_tail_kernel, ff_chunk=1024),
        grid=(N // tm,),
        in_specs=[
            pl.BlockSpec((tm, D), lambda i: (i, 0)),
            pl.BlockSpec((tm, fox2d.shape[1]), lambda i: (i, 0)),
            pl.BlockSpec((tm, ml2d.shape[1]), lambda i: (i, 0)),
            pl.BlockSpec((tm, p2d.shape[1]), lambda i: (i, 0)),
            wspec(wo), wspec(wup), wspec(wdn), wspec(wpg), wspec(wple),
            wspec(gmlp), wspec(gple), wspec(gfin),
        ],
        out_specs=pl.BlockSpec((tm, D), lambda i: (i, 0)),
        out_shape=jax.ShapeDtypeStruct((N, D), F32),
        compiler_params=pltpu.CompilerParams(
            dimension_semantics=("arbitrary",), vmem_limit_bytes=VMEM_LIMIT),
        name="tail",
    )(x2d, fox2d, ml2d, p2d, wo, wup, wdn, wpg, wple, gmlp, gple, gfin)


def _layer(x, p_i, w_in, b_fox_f, b_mlstm_i, b_mlstm_f, w_conv, g_mix, g_fox_out, g_mlstm_out,
           w_out, g_mlp, w_up, w_down, w_ple, g_ple, w_ple_gate, g_final):
    B, S, D = x.shape
    fw, mw = FOX_WIDTH, MLSTM_WIDTH
    o = 0
    cols = {}
    for name, size in (("fq", fw), ("fk", fw), ("fv", fw), ("ff", FOX_HEADS), ("mq", mw), ("mk", mw),
                       ("mv", mw), ("mi", MLSTM_HEADS), ("mf", MLSTM_HEADS), ("mo", mw)):
        cols[name] = w_in[:, o:o + size]
        o += size
    gate_w = jnp.concatenate([cols["ff"], cols["mi"], cols["mf"]], axis=1)
    gate_w = jnp.pad(gate_w, ((0, 0), (0, GATE_COLS - gate_w.shape[1])))
    w_nat = jnp.concatenate([cols["fk"], cols["mq"], cols["mk"], cols["mv"], 0.5 * cols["mo"], gate_w],
                            axis=1).astype(BF16)
    w_t = jnp.concatenate([cols["fq"], cols["fv"]], axis=1).T.astype(BF16)
    bias = jnp.concatenate([b_fox_f, b_mlstm_i, b_mlstm_f])
    bias = jnp.pad(bias, (0, GATE_COLS - bias.shape[0]))[None, :].astype(F32)

    fk, mz, qvt, gm, gmt, caug = _inproj(x, g_mix[None, :], w_nat, w_t, bias, 0.5 * w_conv.astype(F32), tm=512)

    tb = 512
    gain_b = jnp.broadcast_to(g_fox_out.astype(F32)[:, None], (fw, tb))
    fox = _fox(qvt, fk, caug, gain_b, tb=tb)
    ml = _mlstm(mz, gm, gmt, 0.5 * g_mlstm_out[None, :].astype(F32), tm=512)

    out = _tail(x.reshape(B * S, D), fox.reshape(B * S, fw), ml.reshape(B * S, mw),
                p_i.reshape(B * S, p_i.shape[-1]),
                w_out.astype(BF16), w_up.astype(BF16), w_down.astype(BF16),
                w_ple_gate.astype(BF16), w_ple.astype(BF16),
                g_mlp[None, :], g_ple[None, :], g_final[None, :], tm=1024)
    return out.reshape(B, S, D)


def kernel(x, p, w_in, b_fox_f, b_mlstm_i, b_mlstm_f, w_conv, g_mix, g_fox_out, g_mlstm_out, w_out,
           g_mlp, w_up, w_down, w_ple, g_ple, w_ple_gate, g_final):
    assert w_in.shape[0] == 1, "single-layer kernel: the final RMSNorm is fused into the layer"
    return _layer(x, p[0], w_in[0], b_fox_f[0], b_mlstm_i[0], b_mlstm_f[0], w_conv[0], g_mix[0],
                  g_fox_out[0], g_mlstm_out[0], w_out[0], g_mlp[0], w_up[0], w_down[0], w_ple[0],
                  g_ple[0], w_ple_gate[0], g_final)
```
